```python
import math
import jax, jax.numpy as jnp
from jax import lax
import numpy as np

D_MODEL = 1024
BATCH = 2
SEQ = 8192
DEPTH = 4
DEC_BATCH = 32
DEC_SEQ = 4
PAST_LEN = 8192
PAGE_SIZE = 128

MIX_WIDTH = D_MODEL
SSM_WIDTH = MIX_WIDTH // 2
SSM_GROUP_CH = 16
SSM_GROUPS = SSM_WIDTH // SSM_GROUP_CH
SSM_STATE = 64
SSM_CHUNK = 128
ATT_WIDTH = MIX_WIDTH - SSM_WIDTH
HEAD_DIM = 64
ATT_HEADS = ATT_WIDTH // HEAD_DIM
ATT_SCALE = HEAD_DIM ** -0.5
Q_BLOCK = 128
IN_COLS = SSM_WIDTH + 3 * ATT_WIDTH + ATT_HEADS
N_EXPERTS = 32
TOP_K = 4
D_EXPERT = D_MODEL
SWIGLU_LIMIT = 7.0
SWIGLU_ALPHA = 1.702
MOE_BLOCK = 128
N_MOD = 6
NORM_EPS = 1e-6

kernel_name = 'hymba_s5_fox_moe_adaln_step'


def rms_norm(x, g):
    xf = x.astype(jnp.float32)
    y = xf * lax.rsqrt(jnp.mean(xf * xf, axis=-1, keepdims=True) + NORM_EPS)
    return (y * g.astype(jnp.float32)).astype(x.dtype)


def adaln_params(c, w_ada, b_ada):
    mod = jax.nn.silu(c) @ w_ada + b_ada
    return [m[:, None, :] for m in jnp.split(mod, N_MOD, axis=-1)]


def in_project(h, w_in, b_forget):
    z = h @ w_in
    lead = z.shape[:-1]
    u = z[..., :SSM_WIDTH]
    o = SSM_WIDTH
    q = z[..., o:o + ATT_WIDTH].reshape(lead + (ATT_HEADS, HEAD_DIM))
    k = z[..., o + ATT_WIDTH:o + 2 * ATT_WIDTH].reshape(lead + (ATT_HEADS, HEAD_DIM))
    v = z[..., o + 2 * ATT_WIDTH:o + 3 * ATT_WIDTH].reshape(lead + (ATT_HEADS, HEAD_DIM))
    f_logit = z[..., o + 3 * ATT_WIDTH:] + b_forget
    logf = jax.nn.log_sigmoid(f_logit.astype(jnp.float32))
    return u, q, k, v, logf


def complex_affine_combine(earlier, later):
    a1r, a1i, b1r, b1i = earlier
    a2r, a2i, b2r, b2i = later
    ar = a1r * a2r - a1i * a2i
    ai = a1r * a2i + a1i * a2r
    br = a2r * b1r - a2i * b1i + b2r
    bi = a2r * b1i + a2i * b1r + b2i
    return ar, ai, br, bi


def ssm_discretize(lam_re, lam_im, log_dt, b_re, b_im):
    dt = jnp.exp(log_dt)[:, None]
    mag = jnp.exp(lam_re * dt)
    a_re = mag * jnp.cos(lam_im * dt)
    a_im = mag * jnp.sin(lam_im * dt)
    den = lam_re * lam_re + lam_im * lam_im
    nr = a_re - 1.0
    coef_re = (nr * lam_re + a_im * lam_im) / den
    coef_im = (a_im * lam_re - nr * lam_im) / den
    bb_re = coef_re[..., None] * b_re - coef_im[..., None] * b_im
    bb_im = coef_re[..., None] * b_im + coef_im[..., None] * b_re
    return a_re, a_im, bb_re, bb_im


def ssm_block(u, h_re, h_im, disc, c_re, c_im, d_skip):
    a_re, a_im, bb_re, bb_im = disc
    bsz, t = u.shape[:2]
    ug = u.reshape(bsz, t, SSM_GROUPS, SSM_GROUP_CH)
    bu_re = jnp.einsum('btgc,gpc->btgp', ug, bb_re)
    bu_im = jnp.einsum('btgc,gpc->btgp', ug, bb_im)
    bu_re = bu_re.at[:, 0].add(a_re * h_re - a_im * h_im)
    bu_im = bu_im.at[:, 0].add(a_re * h_im + a_im * h_re)
    ar = jnp.broadcast_to(a_re, bu_re.shape)
    ai = jnp.broadcast_to(a_im, bu_im.shape)
    _, _, s_re, s_im = lax.associative_scan(complex_affine_combine, (ar, ai, bu_re, bu_im), axis=1)
    y = jnp.einsum('btgp,gcp->btgc', s_re, c_re) - jnp.einsum('btgp,gcp->btgc', s_im, c_im)
    y = y.reshape(bsz, t, SSM_WIDTH) + d_skip * u
    return y, s_re[:, -1], s_im[:, -1]


def ssm_prompt(u, disc, c_re, c_im, d_skip):
    bsz, seq = u.shape[:2]
    n_chunks = seq // SSM_CHUNK
    dt = jnp.result_type(u.dtype, disc[2].dtype)
    h0 = jnp.zeros((bsz, SSM_GROUPS, SSM_STATE), dt)
    uc = u.reshape(bsz, n_chunks, SSM_CHUNK, SSM_WIDTH).swapaxes(0, 1)

    def step(carry, u_blk):
        y, h_re, h_im = ssm_block(u_blk, carry[0], carry[1], disc, c_re, c_im, d_skip)
        return (h_re, h_im), y

    (h_re, h_im), ys = lax.scan(step, (h0, h0), uc)
    return ys.swapaxes(0, 1).reshape(bsz, seq, SSM_WIDTH), h_re, h_im


def ssm_glu(y, w_glu, b_glu):
    y = jax.nn.gelu(y)
    return y * jax.nn.sigmoid(y @ w_glu + b_glu)


def attend(q, k, v, fq, fk, q_pos, k_pos):
    s = jnp.einsum('bqhd,bkhd->bhqk', q, k).astype(jnp.float32) * ATT_SCALE
    s = s + jnp.swapaxes(fq, 1, 2)[..., :, None] - jnp.swapaxes(fk, 1, 2)[:, :, None, :]
    mask = k_pos[None, :] <= q_pos[:, None]
    s = jnp.where(mask, s, -jnp.inf)
    p = jax.nn.softmax(s, axis=-1)
    return jnp.einsum('bhqk,bkhd->bqhd', p.astype(v.dtype), v)


def attention_prompt(q, k, v, fcum):
    bsz, seq = q.shape[:2]
    nqb = seq // Q_BLOCK
    pos = jnp.arange(seq)
    qb = q.reshape(bsz, nqb, Q_BLOCK, ATT_HEADS, HEAD_DIM).swapaxes(0, 1)
    fb = fcum.reshape(bsz, nqb, Q_BLOCK, ATT_HEADS).swapaxes(0, 1)
    pb = pos.reshape(nqb, Q_BLOCK)
    out = lax.map(lambda a: attend(a[0], k, v, a[1], fcum, a[2], pos), (qb, fb, pb))
    return out.swapaxes(0, 1).reshape(bsz, seq, ATT_HEADS, HEAD_DIM)


def merge_heads(y_ssm, y_att, g_out_ssm, g_out_att, w_out):
    y_att = y_att.reshape(y_att.shape[:-2] + (ATT_WIDTH,))
    return jnp.concatenate([rms_norm(y_ssm, g_out_ssm), rms_norm(y_att, g_out_att)], axis=-1) @ w_out


def mixer_prompt(h, lp):
    u, q, k, v, logf = in_project(h, lp['w_in'], lp['b_forget'])
    q = rms_norm(q, lp['g_q'])
    k = rms_norm(k, lp['g_k'])
    disc = ssm_discretize(lp['lam_re'], lp['lam_im'], lp['log_dt'], lp['b_re'], lp['b_im'])
    y_ssm, s_re, s_im = ssm_prompt(u, disc, lp['c_re'], lp['c_im'], lp['d'])
    y_ssm = ssm_glu(y_ssm, lp['w_glu'], lp['b_glu'])
    fcum = jnp.cumsum(logf, axis=1)
    y_att = attention_prompt(q, k, v, fcum)
    out = merge_heads(y_ssm, y_att, lp['g_out_ssm'], lp['g_out_att'], lp['w_out'])
    return out, (k, v, logf, s_re, s_im)


def mixer_sample(h, lp, k_past, v_past, logf_past, s_re0, s_im0):
    t = h.shape[1]
    past_len = k_past.shape[1]
    u, q, k, v, logf = in_project(h, lp['w_in'], lp['b_forget'])
    q = rms_norm(q, lp['g_q'])
    k = rms_norm(k, lp['g_k'])
    disc = ssm_discretize(lp['lam_re'], lp['lam_im'], lp['log_dt'], lp['b_re'], lp['b_im'])
    y_ssm, s_re, s_im = ssm_block(u, s_re0, s_im0, disc, lp['c_re'], lp['c_im'], lp['d'])
    y_ssm = ssm_glu(y_ssm, lp['w_glu'], lp['b_glu'])
    k_all = jnp.concatenate([k_past.astype(k.dtype), k], axis=1)
    v_all = jnp.concatenate([v_past.astype(v.dtype), v], axis=1)
    fcum = jnp.cumsum(jnp.concatenate([logf_past.astype(jnp.float32), logf], axis=1), axis=1)
    q_pos = past_len + jnp.arange(t)
    k_pos = jnp.arange(past_len + t)
    y_att = attend(q, k_all, v_all, fcum[:, past_len:], fcum, q_pos, k_pos)
    out = merge_heads(y_ssm, y_att, lp['g_out_ssm'], lp['g_out_att'], lp['w_out'])
    return out, (k, v, logf, s_re, s_im)


def moe_ffn(h, w_router, b_router, w_gate_up, b_gate_up, w_down, b_down):
    lead = h.shape[:-1]
    x = h.reshape(-1, D_MODEL)
    n = x.shape[0]
    n_assign = n * TOP_K
    blk = min(MOE_BLOCK, max(1, n_assign // N_EXPERTS))
    n_blocks = -(-n_assign // blk) + N_EXPERTS
    cap = n_blocks * blk
    logits = (x @ w_router + b_router).astype(jnp.float32)
    top_val, top_idx = lax.top_k(logits, TOP_K)
    gate = jax.nn.softmax(top_val, axis=-1)
    e_flat = top_idx.reshape(-1).astype(jnp.int32)
    tok_flat = jnp.arange(n_assign, dtype=jnp.int32) // TOP_K
    g_flat = gate.reshape(-1)
    order = jnp.argsort(e_flat)
    e_s, tok_s, g_s = e_flat[order], tok_flat[order], g_flat[order]
    counts = jnp.bincount(e_flat, length=N_EXPERTS).astype(jnp.int32)
    start = jnp.cumsum(counts) - counts
    padded = (counts + blk - 1) // blk * blk
    pend = jnp.cumsum(padded)
    pstart = pend - padded
    dest = pstart[e_s] + jnp.arange(n_assign, dtype=jnp.int32) - start[e_s]
    tok_pad = jnp.zeros((cap,), jnp.int32).at[dest].set(tok_s)
    g_pad = jnp.zeros((cap,), jnp.float32).at[dest].set(g_s)
    blk_start = jnp.arange(n_blocks, dtype=jnp.int32) * blk
    blk_expert = jnp.minimum(jnp.searchsorted(pend, blk_start, side='right'), N_EXPERTS - 1)
    x_pad = x[tok_pad].reshape(n_blocks, blk, D_MODEL)

    def expert_block(args):
        xb, e = args
        gu = xb @ w_gate_up[e] + b_gate_up[e]
        g_, up = gu[:, :D_EXPERT], gu[:, D_EXPERT:]
        g_ = jnp.minimum(g_, SWIGLU_LIMIT)
        up = jnp.clip(up, -SWIGLU_LIMIT, SWIGLU_LIMIT)
        act = (up + 1.0) * (g_ * jax.nn.sigmoid(SWIGLU_ALPHA * g_))
        return act @ w_down[e] + b_down[e]

    y_pad = lax.map(expert_block, (x_pad, blk_expert)).reshape(cap, D_MODEL)
    y = jnp.zeros((n, D_MODEL), jnp.float32).at[tok_pad].add(y_pad.astype(jnp.float32) * g_pad[:, None])
    return y.astype(h.dtype).reshape(lead + (D_MODEL,))


def trunk_layer(x, mod, g1, g2, mixer_fn, moe_p):
    shift1, scale1, gate1, shift2, scale2, gate2 = mod
    h = rms_norm(x, g1) * (1.0 + scale1) + shift1
    mix, new_state = mixer_fn(h)
    x = x + gate1 * mix
    h = rms_norm(x, g2) * (1.0 + scale2) + shift2
    x = x + gate2 * moe_ffn(h, *moe_p)
    return x, new_state


def setup_inputs(seed: int = 0) -> dict:
    key = jax.random.key(seed)
    ks = iter(jax.random.split(key, 48))
    f32 = jnp.float32

    def nrm(shape, scale):
        return scale * jax.random.normal(next(ks), shape, f32)

    def gain(shape):
        return 1.0 + 0.02 * jax.random.normal(next(ks), shape, f32)

    n_pages = PAST_LEN // PAGE_SIZE
    n_used = DEC_BATCH * n_pages
    n_pool = n_used + max(1, n_used // 4)
    G, P, GC = SSM_GROUPS, SSM_STATE, SSM_GROUP_CH
    inp = {}
    inp['x_prompt'] = nrm((BATCH, SEQ, D_MODEL), 1.0)
    inp['x_sample'] = nrm((DEC_BATCH, DEC_SEQ, D_MODEL), 1.0)
    inp['c_prompt'] = nrm((BATCH, D_MODEL), 1.0)
    inp['c_sample'] = nrm((DEC_BATCH, D_MODEL), 1.0)
    inp['cache_k'] = nrm((DEPTH, n_pool, PAGE_SIZE, ATT_HEADS, HEAD_DIM), 1.0)
    inp['cache_v'] = nrm((DEPTH, n_pool, PAGE_SIZE, ATT_HEADS, HEAD_DIM), 1.0)
    inp['cache_logf'] = jax.nn.log_sigmoid(3.0 + jax.random.normal(next(ks), (DEPTH, n_pool, PAGE_SIZE, ATT_HEADS), f32))
    inp['state_ssm_re'] = nrm((DEPTH, DEC_BATCH, G, P), 0.3)
    inp['state_ssm_im'] = nrm((DEPTH, DEC_BATCH, G, P), 0.3)
    inp['page_table'] = jax.random.permutation(next(ks), n_pool)[:n_used].reshape(DEC_BATCH, n_pages).astype(jnp.int32)
    inp['w_ada'] = nrm((DEPTH, D_MODEL, N_MOD * D_MODEL), 0.5 * D_MODEL ** -0.5)
    inp['b_ada'] = nrm((DEPTH, N_MOD * D_MODEL), 0.02)
    inp['g_norm1'] = gain((DEPTH, D_MODEL))
    inp['g_norm2'] = gain((DEPTH, D_MODEL))
    inp['w_in'] = nrm((DEPTH, D_MODEL, IN_COLS), D_MODEL ** -0.5)
    inp['b_forget'] = jax.random.uniform(next(ks), (DEPTH, ATT_HEADS), f32, 1.0, 5.0)
    inp['g_q'] = gain((DEPTH, HEAD_DIM))
    inp['g_k'] = gain((DEPTH, HEAD_DIM))
    inp['ssm_lam_re'] = -0.5 * jnp.exp(0.05 * jax.random.normal(next(ks), (DEPTH, G, P), f32))
    inp['ssm_lam_im'] = math.pi * jnp.arange(P, dtype=f32) + 0.01 * jax.random.normal(next(ks), (DEPTH, G, P), f32)
    inp['ssm_log_dt'] = jax.random.uniform(next(ks), (DEPTH, G), f32, math.log(1e-3), math.log(1e-1))
    inp['ssm_b_re'] = nrm((DEPTH, G, P, GC), (2 * GC) ** -0.5)
    inp['ssm_b_im'] = nrm((DEPTH, G, P, GC), (2 * GC) ** -0.5)
    inp['ssm_c_re'] = nrm((DEPTH, G, GC, P), P ** -0.5)
    inp['ssm_c_im'] = nrm((DEPTH, G, GC, P), P ** -0.5)
    inp['ssm_d'] = nrm((DEPTH, SSM_WIDTH), 0.3)
    inp['w_glu'] = nrm((DEPTH, SSM_WIDTH, SSM_WIDTH), SSM_WIDTH ** -0.5)
    inp['b_glu'] = nrm((DEPTH, SSM_WIDTH), 0.02)
    inp['g_out_ssm'] = gain((DEPTH, SSM_WIDTH))
    inp['g_out_att'] = gain((DEPTH, ATT_WIDTH))
    inp['w_out'] = nrm((DEPTH, MIX_WIDTH, D_MODEL), MIX_WIDTH ** -0.5)
    inp['w_router'] = nrm((DEPTH, D_MODEL, N_EXPERTS), D_MODEL ** -0.5)
    inp['b_router'] = nrm((DEPTH, N_EXPERTS), 0.01)
    inp['w_gate_up'] = nrm((DEPTH, N_EXPERTS, D_MODEL, 2 * D_EXPERT), D_MODEL ** -0.5)
    inp['b_gate_up'] = nrm((DEPTH, N_EXPERTS, 2 * D_EXPERT), 0.02)
    inp['w_down'] = nrm((DEPTH, N_EXPERTS, D_EXPERT, D_MODEL), D_EXPERT ** -0.5)
    inp['b_down'] = nrm((DEPTH, N_EXPERTS, D_MODEL), 0.02)
    return inp


def reference(x_prompt, x_sample, c_prompt, c_sample, cache_k, cache_v, cache_logf,
              state_ssm_re, state_ssm_im, page_table,
              w_ada, b_ada, g_norm1, g_norm2, w_in, b_forget, g_q, g_k,
              ssm_lam_re, ssm_lam_im, ssm_log_dt, ssm_b_re, ssm_b_im, ssm_c_re, ssm_c_im, ssm_d,
              w_glu, b_glu, g_out_ssm, g_out_att, w_out,
              w_router, b_router, w_gate_up, b_gate_up, w_down, b_down):
    dec_b, n_pages = page_table.shape
    past_len = n_pages * PAGE_SIZE
    xp, xs = x_prompt, x_sample
    st_p, st_s = [], []
    for l in range(DEPTH):
        lp = {'w_in': w_in[l], 'b_forget': b_forget[l], 'g_q': g_q[l], 'g_k': g_k[l],
              'lam_re': ssm_lam_re[l], 'lam_im': ssm_lam_im[l], 'log_dt': ssm_log_dt[l],
              'b_re': ssm_b_re[l], 'b_im': ssm_b_im[l], 'c_re': ssm_c_re[l], 'c_im': ssm_c_im[l],
              'd': ssm_d[l], 'w_glu': w_glu[l], 'b_glu': b_glu[l],
              'g_out_ssm': g_out_ssm[l], 'g_out_att': g_out_att[l], 'w_out': w_out[l]}
        moe_p = (w_router[l], b_router[l], w_gate_up[l], b_gate_up[l], w_down[l], b_down[l])
        mod_p = adaln_params(c_prompt, w_ada[l], b_ada[l])
        xp, sp = trunk_layer(xp, mod_p, g_norm1[l], g_norm2[l], lambda h: mixer_prompt(h, lp), moe_p)
        st_p.append(sp)
        k_past = cache_k[l, page_table].reshape(dec_b, past_len, ATT_HEADS, HEAD_DIM)
        v_past = cache_v[l, page_table].reshape(dec_b, past_len, ATT_HEADS, HEAD_DIM)
        lf_past = cache_logf[l, page_table].reshape(dec_b, past_len, ATT_HEADS)
        mod_s = adaln_params(c_sample, w_ada[l], b_ada[l])
        xs, ss = trunk_layer(xs, mod_s, g_norm1[l], g_norm2[l],
                             lambda h: mixer_sample(h, lp, k_past, v_past, lf_past,
                                                    state_ssm_re[l], state_ssm_im[l]), moe_p)
        st_s.append(ss)
    k_prompt = jnp.stack([s[0] for s in st_p])
    v_prompt = jnp.stack([s[1] for s in st_p])
    logf_prompt = jnp.stack([s[2] for s in st_p])
    ssm_re_prompt = jnp.stack([s[3] for s in st_p])
    ssm_im_prompt = jnp.stack([s[4] for s in st_p])
    k_sample = jnp.stack([s[0] for s in st_s])
    v_sample = jnp.stack([s[1] for s in st_s])
    logf_sample = jnp.stack([s[2] for s in st_s])
    ssm_re_sample = jnp.stack([s[3] for s in st_s])
    ssm_im_sample = jnp.stack([s[4] for s in st_s])
    return (xp, xs, k_prompt, v_prompt, logf_prompt, ssm_re_prompt, ssm_im_prompt,
            k_sample, v_sample, logf_sample, ssm_re_sample, ssm_im_sample)
```

```python
import functools
import math

import jax
import jax.numpy as jnp
import numpy as np
from jax import lax
from jax.experimental import pallas as pl
from jax.experimental.pallas import tpu as pltpu

F32 = jnp.float32
BF16 = jnp.bfloat16

D_MODEL = 1024
DEPTH = 4
PAGE_SIZE = 128
SSM_WIDTH = 512
SSM_GROUP_CH = 16
SSM_GROUPS = 32
SSM_STATE = 64
ATT_WIDTH = 512
HEAD_DIM = 64
ATT_HEADS = 8
ATT_SCALE = HEAD_DIM ** -0.5
IN_MAIN = SSM_WIDTH + 3 * ATT_WIDTH
N_EXPERTS = 32
TOP_K = 4
D_EXPERT = 1024
SWIGLU_LIMIT = 7.0
SWIGLU_ALPHA = 1.702
N_MOD = 6
NORM_EPS = 1e-6

LANES = 128
SSM_CHUNK = 8
SSM_BLOCKS = SSM_WIDTH // LANES
GROUPS_PER_BLOCK = LANES // SSM_GROUP_CH
STATES_PER_BLOCK = GROUPS_PER_BLOCK * SSM_STATE
VMEM_LIMIT = 56 * 1024 * 1024


def _cparams(sem):
    return pltpu.CompilerParams(dimension_semantics=sem, vmem_limit_bytes=VMEM_LIMIT)


def _split2(x):
    hi = x.astype(BF16)
    lo = (x - hi.astype(F32)).astype(BF16)
    return hi, lo


def _split3(x):
    hi = x.astype(BF16)
    r = x - hi.astype(F32)
    mid = r.astype(BF16)
    lo = (r - mid.astype(F32)).astype(BF16)
    return hi, mid, lo


def _dot(a, b):
    return jnp.dot(a, b, preferred_element_type=F32)


def _dot_x3(a, b):
    ah, al = _split2(a)
    bh, bl = _split2(b)
    return _dot(ah, bh) + (_dot(ah, bl) + _dot(al, bh))


def _dot_nt(a, b):
    return lax.dot_general(a, b, (((1,), (1,)), ((), ())), preferred_element_type=F32)


def _rms(x, g):
    return x * lax.rsqrt(jnp.mean(x * x, axis=-1, keepdims=True) + NORM_EPS) * g


def _adaln_kernel(c_ref, w_ref, b_ref, o_ref):
    c = c_ref[...]
    a = c * jax.nn.sigmoid(c)
    o_ref[...] = _dot_x3(a, w_ref[...]) + b_ref[...]


def _adaln(c_all, w_ada, b_ada):
    rows = c_all.shape[0]
    tn = 1536
    nt = (N_MOD * D_MODEL) // tn
    return pl.pallas_call(
        _adaln_kernel,
        grid=(DEPTH, nt),
        in_specs=[
            pl.BlockSpec((rows, D_MODEL), lambda l, n: (0, 0)),
            pl.BlockSpec((None, D_MODEL, tn), lambda l, n: (l, 0, n)),
            pl.BlockSpec((None, 1, tn), lambda l, n: (l, 0, n)),
        ],
        out_specs=pl.BlockSpec((None, rows, tn), lambda l, n: (l, 0, n)),
        out_shape=jax.ShapeDtypeStruct((DEPTH, rows, N_MOD * D_MODEL), F32),
        compiler_params=_cparams(("arbitrary", "arbitrary")),
        name="adaln",
    )(c_all, w_ada, b_ada.reshape(DEPTH, 1, N_MOD * D_MODEL))


def _log_sigmoid(x):
    return jnp.minimum(x, 0.0) - jnp.log1p(jnp.exp(-jnp.abs(x)))


def _inproj_kernel(x_ref, shift_ref, scale_ref, g1_ref, wm_ref, wf_ref, bf_ref, gq_ref, gk_ref,
                   seg_ref, tri_ref,
                   u_ref, q_ref, k_ref, v_ref, kb_ref, vb_ref, lf_ref, fc_ref, carry_ref,
                   *, hi_prec, with_cumsum):
    x = x_ref[...]
    h = _rms(x, g1_ref[...]) * (1.0 + scale_ref[...]) + shift_ref[...]
    if hi_prec:
        z = _dot_x3(h, wm_ref[...])
        zf = _dot_x3(h, wf_ref[...])
    else:
        hb = h.astype(BF16)
        z = _dot(hb, wm_ref[...])
        zf = _dot(hb, wf_ref[...])
    u_ref[...] = z[:, :SSM_WIDTH]
    q = z[:, SSM_WIDTH:SSM_WIDTH + ATT_WIDTH]
    k = z[:, SSM_WIDTH + ATT_WIDTH:SSM_WIDTH + 2 * ATT_WIDTH]
    v = z[:, SSM_WIDTH + 2 * ATT_WIDTH:]
    seg = seg_ref[...]

    def head_norm(t, g):
        hi, lo = _split2(t * t)
        ms = _dot(hi, seg) + _dot(lo, seg)
        return t * lax.rsqrt(ms + NORM_EPS) * g

    qn = head_norm(q, gq_ref[...])
    kn = head_norm(k, gk_ref[...])
    q_ref[...] = (qn * ATT_SCALE).astype(BF16)
    k_ref[...] = kn
    kb_ref[...] = kn.astype(BF16)
    v_ref[...] = v
    vb_ref[...] = v.astype(BF16)
    logf = _log_sigmoid(zf + bf_ref[...])
    lf_ref[...] = logf
    if with_cumsum:
        @pl.when(pl.program_id(1) == 0)
        def _():
            carry_ref[...] = jnp.zeros_like(carry_ref)

        tri = tri_ref[...]
        hi, mid, lo = _split3(logf)
        fc = carry_ref[...] + (_dot(tri, hi) + (_dot(tri, mid) + _dot(tri, lo)))
        fc_ref[...] = fc
        carry_ref[...] = fc[-1:, :]
    else:
        fc_ref[...] = logf


def _seg_matrix():
    i = np.arange(ATT_WIDTH) // HEAD_DIM
    return jnp.asarray((i[:, None] == i[None, :]).astype(np.float32) / HEAD_DIM, dtype=BF16)


def _tri_matrix(n, strict=False):
    i = np.arange(n)
    m = (i[None, :] < i[:, None]) if strict else (i[None, :] <= i[:, None])
    return jnp.asarray(m.astype(np.float32), dtype=BF16)


def _in_proj(x, shift, scale, g1, w_main, w_f, b_f, g_q, g_k, *, tile, hi_prec, with_cumsum):
    s_dim, r_dim, _ = x.shape
    rm = shift.shape[1]
    nt = r_dim // tile
    row = lambda s, t: (s, t, 0)
    const2 = lambda s, t: (0, 0)
    mod_map = (lambda s, t: (s, 0, 0)) if rm == 1 else row
    mod_blk = (None, 1, D_MODEL) if rm == 1 else (None, tile, D_MODEL)
    outs = [
        jax.ShapeDtypeStruct((s_dim, r_dim, SSM_WIDTH), F32),
        jax.ShapeDtypeStruct((s_dim, r_dim, ATT_WIDTH), BF16),
        jax.ShapeDtypeStruct((s_dim, r_dim, ATT_WIDTH), F32),
        jax.ShapeDtypeStruct((s_dim, r_dim, ATT_WIDTH), F32),
        jax.ShapeDtypeStruct((s_dim, r_dim, ATT_WIDTH), BF16),
        jax.ShapeDtypeStruct((s_dim, r_dim, ATT_WIDTH), BF16),
        jax.ShapeDtypeStruct((s_dim, r_dim, LANES), F32),
        jax.ShapeDtypeStruct((s_dim, r_dim, LANES), F32),
    ]
    wide = pl.BlockSpec((None, tile, ATT_WIDTH), row)
    narrow = pl.BlockSpec((None, tile, LANES), row)
    return pl.pallas_call(
        functools.partial(_inproj_kernel, hi_prec=hi_prec, with_cumsum=with_cumsum),
        grid=(s_dim, nt),
        in_specs=[
            pl.BlockSpec((None, tile, D_MODEL), row),
            pl.BlockSpec(mod_blk, mod_map),
            pl.BlockSpec(mod_blk, mod_map),
            pl.BlockSpec((1, D_MODEL), const2),
            pl.BlockSpec(w_main.shape, const2),
            pl.BlockSpec(w_f.shape, const2),
            pl.BlockSpec((1, LANES), const2),
            pl.BlockSpec((1, ATT_WIDTH), const2),
            pl.BlockSpec((1, ATT_WIDTH), const2),
            pl.BlockSpec((ATT_WIDTH, ATT_WIDTH), const2),
            pl.BlockSpec((tile, tile), const2),
        ],
        out_specs=[wide, wide, wide, wide, wide, wide, narrow, narrow],
        out_shape=outs,
        scratch_shapes=[pltpu.VMEM((1, LANES), F32)],
        compiler_params=_cparams(("arbitrary", "arbitrary")),
        name="in_proj_hi" if hi_prec else "in_proj",
    )(x, shift, scale, g1, w_main, w_f, b_f, g_q, g_k, _seg_matrix(), _tri_matrix(tile))


def _discretize(lam_re, lam_im, log_dt):
    dt = jnp.exp(log_dt)
    mag = jnp.exp(lam_re * dt)
    ang = lam_im * dt
    a_re = mag * jnp.cos(ang)
    a_im = mag * jnp.sin(ang)
    den = lam_re * lam_re + lam_im * lam_im
    nr = a_re - 1.0
    coef_re = (nr * lam_re + a_im * lam_im) / den
    coef_im = (a_im * lam_re - nr * lam_im) / den
    return a_re, a_im, coef_re, coef_im


def _powers(a_re, a_im, n):
    pr, pi = [jnp.ones_like(a_re)], [jnp.zeros_like(a_im)]
    for _ in range(n):
        r, i = pr[-1], pi[-1]
        pr.append(r * a_re - i * a_im)
        pi.append(r * a_im + i * a_re)
    return pr, pi


def _ssm_prep_kernel(lr_row, li_row, dt_row, lr_col, li_col, dt_col, btr_ref, bti_ref, ctr_ref, cti_ref,
                     d_ref, tmat_ref, wz_ref, wc_ref, apow_ref, bbt_ref):
    L, W = SSM_CHUNK, LANES
    ar, ai, cr, ci = _discretize(lr_row[...], li_row[...], dt_row[...])
    acr, aci, _, _ = _discretize(lr_col[...], li_col[...], dt_col[...])
    btr, bti = btr_ref[...], bti_ref[...]
    bb_re = cr * btr - ci * bti
    bb_im = cr * bti + ci * btr
    bbt_ref[0] = bb_re
    bbt_ref[1] = bb_im
    ctr, cti = ctr_ref[...], cti_ref[...]
    pr, pi = _powers(ar, ai, L)
    pcr, pci = _powers(acr, aci, L)
    apow_ref[...] = jnp.concatenate(
        [ar, ai, pr[L], pi[L], jnp.zeros((4, STATES_PER_BLOCK), F32)], axis=0)
    eye = (lax.broadcasted_iota(jnp.int32, (W, W), 0) == lax.broadcasted_iota(jnp.int32, (W, W), 1))
    kts = []
    for tau in range(L):
        m_re = pr[tau] * bb_re - pi[tau] * bb_im
        m_im = pr[tau] * bb_im + pi[tau] * bb_re
        kt = _dot_x3(m_re, ctr) - _dot_x3(m_im, cti)
        if tau == 0:
            kt = kt + jnp.where(eye, d_ref[...], 0.0)
        kts.append(kt.astype(BF16))
        jp = L - 1 - tau
        wz_ref[jp * W:(jp + 1) * W, 0:STATES_PER_BLOCK] = m_re.astype(BF16)
        wz_ref[jp * W:(jp + 1) * W, STATES_PER_BLOCK:] = m_im.astype(BF16)
    zero = jnp.zeros((W, W), BF16)
    for jp in range(L):
        for j in range(L):
            tmat_ref[jp * W:(jp + 1) * W, j * W:(j + 1) * W] = kts[j - jp] if j >= jp else zero
    for j in range(L):
        qr, qi = pcr[j + 1], pci[j + 1]
        wc_ref[0:STATES_PER_BLOCK, j * W:(j + 1) * W] = (ctr * qr - cti * qi).astype(BF16)
        wc_ref[STATES_PER_BLOCK:, j * W:(j + 1) * W] = (-(ctr * qi + cti * qr)).astype(BF16)


def _blockdiag(t):
    a, b = t.shape[-2:]
    t = t.reshape(DEPTH, SSM_BLOCKS, GROUPS_PER_BLOCK, a, b)
    eye = jnp.eye(GROUPS_PER_BLOCK, dtype=t.dtype)
    out = jnp.einsum('lkgab,gh->lkgahb', t, eye)
    return out.reshape(DEPTH, SSM_BLOCKS, GROUPS_PER_BLOCK * a, GROUPS_PER_BLOCK * b)


def _ssm_prep(lam_re, lam_im, log_dt, b_re, b_im, c_re, c_im, d):
    S, KW = STATES_PER_BLOCK, SSM_CHUNK * LANES
    ldt = jnp.broadcast_to(log_dt[:, :, None], lam_re.shape)
    rows = [t.reshape(DEPTH, SSM_BLOCKS, 1, S) for t in (lam_re, lam_im, ldt)]
    cols = [t.reshape(DEPTH, SSM_BLOCKS, S, 1) for t in (lam_re, lam_im, ldt)]
    bt = [_blockdiag(jnp.swapaxes(t, -1, -2)) for t in (b_re, b_im)]
    ct = [_blockdiag(jnp.swapaxes(t, -1, -2)) for t in (c_re, c_im)]
    d4 = d.reshape(DEPTH, SSM_BLOCKS, 1, LANES)
    blk = lambda *s: pl.BlockSpec((None, None) + s, lambda l, k: (l, k) + (0,) * len(s))
    big = jax.ShapeDtypeStruct((DEPTH, SSM_BLOCKS, KW, KW), BF16)
    outs = pl.pallas_call(
        _ssm_prep_kernel,
        grid=(DEPTH, SSM_BLOCKS),
        in_specs=[blk(1, S)] * 3 + [blk(S, 1)] * 3 + [blk(LANES, S)] * 2 + [blk(S, LANES)] * 2
                 + [blk(1, LANES)],
        out_specs=[blk(KW, KW), blk(KW, KW), blk(KW, KW), blk(8, S), blk(2, LANES, S)],
        out_shape=[big, big, big,
                   jax.ShapeDtypeStruct((DEPTH, SSM_BLOCKS, 8, S), F32),
                   jax.ShapeDtypeStruct((DEPTH, SSM_BLOCKS, 2, LANES, S), F32)],
        compiler_params=_cparams(("arbitrary", "arbitrary")),
        name="ssm_prep",
    )(*rows, *cols, *bt, *ct, d4)
    tmat, wz, wc, apow, bbt = outs
    return dict(tmat=tmat, wz=wz, wc=wc, apow=apow, bbt=bbt, ct_re=ct[0], ct_im=ct[1], d4=d4)


SSM_SEQ_TILE = 2048
SSM_ROWS = SSM_SEQ_TILE // SSM_CHUNK


def _ssm_prompt_kernel(u_ref, tmat_ref, wz_ref, wc_ref, apow_ref, y_ref, st_ref,
                       up_scr, z_scr, h_scr, carry_scr):
    L, W, S, R = SSM_CHUNK, LANES, STATES_PER_BLOCK, SSM_ROWS

    @pl.when(pl.program_id(2) == 0)
    def _():
        carry_scr[...] = jnp.zeros_like(carry_scr)

    for j in range(L):
        up_scr[:, j * W:(j + 1) * W] = u_ref[pl.ds(j, R, stride=L), :].astype(BF16)
    up = up_scr[...]
    z_scr[...] = _dot(up, wz_ref[...])
    a8r = apow_ref[2:3, :]
    a8i = apow_ref[3:4, :]

    def step(m, carry):
        hr, hi = carry
        h_scr[pl.ds(m, 1), 0:S] = hr
        h_scr[pl.ds(m, 1), S:] = hi
        zr = z_scr[pl.ds(m, 1), 0:S]
        zi = z_scr[pl.ds(m, 1), S:]
        return a8r * hr - a8i * hi + zr, a8r * hi + a8i * hr + zi

    hr, hi = lax.fori_loop(0, R, step, (carry_scr[0:1, :], carry_scr[1:2, :]))
    carry_scr[0:1, :] = hr
    carry_scr[1:2, :] = hi
    st_ref[...] = jnp.concatenate([hr, hi, jnp.zeros((6, S), F32)], axis=0)
    y = _dot(up, tmat_ref[...]) + _dot(h_scr[...].astype(BF16), wc_ref[...])
    for j in range(L):
        y_ref[pl.ds(j, R, stride=L), :] = y[:, j * W:(j + 1) * W]


def _ssm_prompt(u, prep, l):
    bsz, seq, _ = u.shape
    KW, S = SSM_CHUNK * LANES, STATES_PER_BLOCK
    wspec = pl.BlockSpec((None, None, KW, KW), lambda k, b, s: (l, k, 0, 0))
    y, st = pl.pallas_call(
        _ssm_prompt_kernel,
        grid=(SSM_BLOCKS, bsz, seq // SSM_SEQ_TILE),
        in_specs=[
            pl.BlockSpec((None, SSM_SEQ_TILE, LANES), lambda k, b, s: (b, s, k)),
            wspec, wspec, wspec,
            pl.BlockSpec((None, None, 8, S), lambda k, b, s: (l, k, 0, 0)),
        ],
        out_specs=[
            pl.BlockSpec((None, SSM_SEQ_TILE, LANES), lambda k, b, s: (b, s, k)),
            pl.BlockSpec((None, None, 8, S), lambda k, b, s: (b, k, 0, 0)),
        ],
        out_shape=[jax.ShapeDtypeStruct((bsz, seq, SSM_WIDTH), F32),
                   jax.ShapeDtypeStruct((bsz, SSM_BLOCKS, 8, S), F32)],
        scratch_shapes=[pltpu.VMEM((SSM_ROWS, KW), BF16), pltpu.VMEM((SSM_ROWS, KW), F32),
                        pltpu.VMEM((SSM_ROWS, KW), F32), pltpu.VMEM((2, S), F32)],
        compiler_params=_cparams(("arbitrary", "arbitrary", "arbitrary")),
        name="ssm_prompt",
    )(u, prep['tmat'], prep['wz'], prep['wc'], prep['apow'])
    s_re = st[:, :, 0, :].reshape(bsz, SSM_GROUPS, SSM_STATE)
    s_im = st[:, :, 1, :].reshape(bsz, SSM_GROUPS, SSM_STATE)
    return y, s_re, s_im


def _ssm_sample_kernel(u_ref, sre_ref, sim_ref, apow_ref, bbt_ref, ctr_ref, cti_ref, d_ref,
                       y_ref, ore_ref, oim_ref):
    a_re = apow_ref[0:1, :]
    a_im = apow_ref[1:2, :]
    bb_re, bb_im = bbt_ref[0], bbt_ref[1]
    ctr, cti = ctr_ref[...], cti_ref[...]
    s_re, s_im = sre_ref[...], sim_ref[...]
    for j in range(u_ref.shape[0]):
        u = u_ref[j]
        n_re = a_re * s_re - a_im * s_im + _dot_x3(u, bb_re)
        n_im = a_re * s_im + a_im * s_re + _dot_x3(u, bb_im)
        s_re, s_im = n_re, n_im
        y_ref[j] = _dot_x3(s_re, ctr) - _dot_x3(s_im, cti) + d_ref[...] * u
    ore_ref[...] = s_re
    oim_ref[...] = s_im


def _ssm_sample(u, state_re, state_im, prep, l):
    t, db, _ = u.shape
    S = STATES_PER_BLOCK
    lk = lambda *s: pl.BlockSpec((None, None) + s, lambda k: (l, k) + (0,) * len(s))
    st_in = pl.BlockSpec((None, db, S), lambda k: (l, 0, k))
    st_out = pl.BlockSpec((db, S), lambda k: (0, k))
    useq = pl.BlockSpec((t, db, LANES), lambda k: (0, 0, k))
    return pl.pallas_call(
        _ssm_sample_kernel,
        grid=(SSM_BLOCKS,),
        in_specs=[useq, st_in, st_in, lk(8, S), lk(2, LANES, S), lk(S, LANES), lk(S, LANES), lk(1, LANES)],
        out_specs=[useq, st_out, st_out],
        out_shape=[jax.ShapeDtypeStruct((t, db, SSM_WIDTH), F32),
                   jax.ShapeDtypeStruct((db, SSM_GROUPS * SSM_STATE), F32),
                   jax.ShapeDtypeStruct((db, SSM_GROUPS * SSM_STATE), F32)],
        compiler_params=_cparams(("arbitrary",)),
        name="ssm_sample",
    )(u, state_re, state_im, prep['apow'], prep['bbt'], prep['ct_re'], prep['ct_im'], prep['d4'])


ATT_TILE = 512
NEG_BIG = -1e30


def _attn_prompt_kernel(qi_ref, kj_ref, q_ref, k_ref, v_ref, fq_ref, fk_ref, o_ref,
                        qm_scr, fq_scr, m_scr, l_scr, acc_scr):
    pair = pl.program_id(1)
    step = pl.program_id(2)
    i = qi_ref[step]
    j = kj_ref[step]
    tq = q_ref.shape[0]
    lane = lax.broadcasted_iota(jnp.int32, (tq, LANES), 1)

    @pl.when(j == 0)
    def _():
        q = q_ref[...]
        fq = fq_ref[...]
        for hh in range(2):
            in_head = (lane >= HEAD_DIM * hh) & (lane < HEAD_DIM * (hh + 1))
            qm_scr[hh] = jnp.where(in_head, q, jnp.zeros_like(q))
            fq_scr[hh] = jnp.sum(jnp.where(lane == 2 * pair + hh, fq, 0.0), axis=1, keepdims=True)
            m_scr[hh] = jnp.full((tq, 1), NEG_BIG, F32)
            l_scr[hh] = jnp.zeros((tq, 1), F32)
            acc_scr[hh] = jnp.zeros((tq, LANES), F32)

    k = k_ref[...]
    v = v_ref[...]
    row = lax.broadcasted_iota(jnp.int32, (tq, tq), 0)
    col = lax.broadcasted_iota(jnp.int32, (tq, tq), 1)
    visible = (col <= row) | (j < i)
    for hh in range(2):
        fk = fk_ref[pl.ds(2 * pair + hh, 1), :]
        s = _dot_nt(qm_scr[hh], k) + (fq_scr[hh] - fk)
        s = jnp.where(visible, s, NEG_BIG)
        m_prev = m_scr[hh]
        m_new = jnp.maximum(m_prev, jnp.max(s, axis=1, keepdims=True))
        alpha = jnp.exp(m_prev - m_new)
        p = jnp.exp(s - m_new)
        l_scr[hh] = alpha * l_scr[hh] + jnp.sum(p, axis=1, keepdims=True)
        acc_scr[hh] = alpha * acc_scr[hh] + _dot(p.astype(BF16), v)
        m_scr[hh] = m_new

    @pl.when(j == i)
    def _():
        o_ref[...] = jnp.where(lane < HEAD_DIM, acc_scr[0] / l_scr[0], acc_scr[1] / l_scr[1])


def _attn_prompt(q, kb, vb, fcum, fcum_t):
    bsz, seq, _ = q.shape
    t = ATT_TILE
    n = seq // t
    qi = np.concatenate([np.full(i + 1, i, np.int32) for i in range(n)])
    kj = np.concatenate([np.arange(i + 1, dtype=np.int32) for i in range(n)])
    qmap = lambda b, p, s, qi, kj: (b, qi[s], p)
    kmap = lambda b, p, s, qi, kj: (b, kj[s], p)
    grid_spec = pltpu.PrefetchScalarGridSpec(
        num_scalar_prefetch=2,
        grid=(bsz, ATT_HEADS // 2, len(qi)),
        in_specs=[
            pl.BlockSpec((None, t, LANES), qmap),
            pl.BlockSpec((None, t, LANES), kmap),
            pl.BlockSpec((None, t, LANES), kmap),
            pl.BlockSpec((None, t, LANES), lambda b, p, s, qi, kj: (b, qi[s], 0)),
            pl.BlockSpec((None, ATT_HEADS, t), lambda b, p, s, qi, kj: (b, 0, kj[s])),
        ],
        out_specs=pl.BlockSpec((None, t, LANES), qmap),
        scratch_shapes=[pltpu.VMEM((2, t, LANES), BF16), pltpu.VMEM((2, t, 1), F32),
                        pltpu.VMEM((2, t, 1), F32), pltpu.VMEM((2, t, 1), F32),
                        pltpu.VMEM((2, t, LANES), F32)],
    )
    return pl.pallas_call(
        _attn_prompt_kernel,
        grid_spec=grid_spec,
        out_shape=jax.ShapeDtypeStruct((bsz, seq, ATT_WIDTH), F32),
        compiler_params=_cparams(("arbitrary", "arbitrary", "arbitrary")),
        name="attn_prompt",
    )(jnp.asarray(qi), jnp.asarray(kj), q, kb, vb, fcum, fcum_t)


PAGES_PER_STEP = 8
FLAT = PAGE_SIZE * ATT_HEADS


def _flat_cumsum_mats():
    c = np.arange(LANES)
    same = (c[:, None] % ATT_HEADS) == (c[None, :] % ATT_HEADS)
    within = same & (c[:, None] <= c[None, :])
    r = np.arange(8)
    strict = (r[None, :] < r[:, None])
    last = np.zeros((LANES, LANES), np.float32)
    for cc in range(LANES):
        last[LANES - ATT_HEADS + cc % ATT_HEADS, cc] = 1.0
    to_bf = lambda m: jnp.asarray(np.asarray(m, np.float32), dtype=BF16)
    return to_bf(same), to_bf(within), to_bf(strict), to_bf(last)


def _fpast_kernel(pt_ref, *refs):
    n = PAGES_PER_STEP
    lf_refs = refs[:n]
    same_ref, within_ref, strict_ref, last_ref, f_ref, tot_ref, carry_scr = refs[n:]

    @pl.when(pl.program_id(1) == 0)
    def _():
        carry_scr[...] = jnp.zeros_like(carry_scr)

    def x3(a, b):
        hi, mid, lo = _split3(a)
        return _dot(hi, b) + (_dot(mid, b) + _dot(lo, b))

    def x3l(b, a):
        hi, mid, lo = _split3(a)
        return _dot(b, hi) + (_dot(b, mid) + _dot(b, lo))

    carry = carry_scr[...]
    for pg in range(n):
        x = lf_refs[pg][...]
        inrow = x3(x, within_ref[...])
        rowsum = x3(x, same_ref[...])
        f = inrow + x3l(strict_ref[...], rowsum) + carry
        f_ref[pg] = f
        carry = x3(f[7:8, :], last_ref[...])
    carry_scr[...] = carry
    tot_ref[...] = jnp.broadcast_to(carry, tot_ref.shape)


def _fpast(logf_view, page_table, l):
    db, n_pages = page_table.shape
    n = PAGES_PER_STEP
    pmap = lambda i: (lambda b, c, pt: (l, pt[b * n_pages + c * n + i], 0, 0))
    const = lambda b, c, pt: (0, 0)
    grid_spec = pltpu.PrefetchScalarGridSpec(
        num_scalar_prefetch=1,
        grid=(db, n_pages // n),
        in_specs=[pl.BlockSpec((None, None, 8, LANES), pmap(i)) for i in range(n)]
                 + [pl.BlockSpec((LANES, LANES), const)] * 2 + [pl.BlockSpec((8, 8), const)]
                 + [pl.BlockSpec((LANES, LANES), const)],
        out_specs=[pl.BlockSpec((None, n, 8, LANES), lambda b, c, pt: (b, c, 0, 0)),
                   pl.BlockSpec((None, 8, LANES), lambda b, c, pt: (b, 0, 0))],
        scratch_shapes=[pltpu.VMEM((1, LANES), F32)],
    )
    same, within, strict, last = _flat_cumsum_mats()
    return pl.pallas_call(
        _fpast_kernel,
        grid_spec=grid_spec,
        out_shape=[jax.ShapeDtypeStruct((db, n_pages, 8, LANES), F32),
                   jax.ShapeDtypeStruct((db, 8, LANES), F32)],
        compiler_params=_cparams(("arbitrary", "arbitrary")),
        name="fpast",
    )(page_table.reshape(-1), *([logf_view] * n), same, within, strict, last)


def _attn_sample_kernel(pt_ref, *refs):
    n = PAGES_PER_STEP
    k_refs, v_refs = refs[:n], refs[n:2 * n]
    (f_ref, tot_ref, q_ref, kn_ref, vn_ref, lfn_ref, hmask_ref, nmask_ref, ncum_ref,
     o_ref, m_scr, l_scr, acc_scr) = refs[2 * n:]
    c = pl.program_id(1)
    nq = q_ref.shape[0]

    @pl.when(c == 0)
    def _():
        m_scr[...] = jnp.full(m_scr.shape, NEG_BIG, F32)
        l_scr[...] = jnp.zeros_like(l_scr)
        acc_scr[...] = jnp.zeros_like(acc_scr)

    q = q_ref[...]
    hmask = hmask_ref[...]
    scores = []
    for pg in range(n):
        k2 = k_refs[pg][...].reshape(FLAT, HEAD_DIM).astype(BF16)
        f = f_ref[pg]
        fk = jnp.concatenate([f[r:r + 1, :] for r in range(8)], axis=1)
        scores.append(_dot_nt(q, k2) - fk + hmask)
    m_blk = functools.reduce(jnp.maximum, scores)
    m_prev = m_scr[...]
    m_new = jnp.maximum(m_prev, jnp.max(m_blk, axis=1, keepdims=True))
    alpha = jnp.exp(m_prev - m_new)
    l_new = alpha * l_scr[...]
    acc = alpha * acc_scr[...]
    for pg in range(n):
        p = jnp.exp(scores[pg] - m_new)
        l_new = l_new + jnp.sum(p, axis=1, keepdims=True)
        v2 = v_refs[pg][...].reshape(FLAT, HEAD_DIM).astype(BF16)
        acc = acc + _dot(p.astype(BF16), v2)
    m_scr[...] = m_new
    l_scr[...] = l_new
    acc_scr[...] = acc

    @pl.when(c == pl.num_programs(1) - 1)
    def _():
        hi, mid, lo = _split3(lfn_ref[...])
        ncum = ncum_ref[...]
        fnew = _dot(hi, ncum) + (_dot(mid, ncum) + _dot(lo, ncum)) + tot_ref[0:1, 0:nq]
        kn = kn_ref[...].astype(BF16)
        s = _dot_nt(q_ref[...], kn) - fnew + nmask_ref[...]
        m_prev = m_scr[...]
        m_new = jnp.maximum(m_prev, jnp.max(s, axis=1, keepdims=True))
        alpha = jnp.exp(m_prev - m_new)
        p = jnp.exp(s - m_new)
        l_fin = alpha * l_scr[...] + jnp.sum(p, axis=1, keepdims=True)
        acc_fin = alpha * acc_scr[...] + _dot(p.astype(BF16), vn_ref[...].astype(BF16))
        o_ref[...] = acc_fin / l_fin


def _attn_sample(q32, kn32, vn32, lfn, cache_k, cache_v, fpast, ftot, page_table, l):
    db, n_pages = page_table.shape
    n = PAGES_PER_STEP
    nq = q32.shape[1]
    t_new = nq // ATT_HEADS
    r = np.arange(nq)
    cflat = np.arange(FLAT)
    hmask = np.where((cflat[None, :] % ATT_HEADS) == (r[:, None] % ATT_HEADS), 0.0, NEG_BIG).astype(np.float32)
    ok = ((r[None, :] % ATT_HEADS) == (r[:, None] % ATT_HEADS)) & ((r[None, :] // ATT_HEADS) <= (r[:, None] // ATT_HEADS))
    nmask = np.where(ok, 0.0, NEG_BIG).astype(np.float32)
    ncum = (((r[:, None] % ATT_HEADS) == (r[None, :] % ATT_HEADS)) & (r[:, None] <= r[None, :])).astype(np.float32)
    pmap = lambda i: (lambda b, c, pt: (l, pt[b * n_pages + c * n + i], 0, 0, 0))
    bmap = lambda b, c, pt: (b, 0, 0)
    const = lambda b, c, pt: (0, 0)
    page = lambda i: pl.BlockSpec((None, None, PAGE_SIZE, ATT_HEADS, HEAD_DIM), pmap(i))
    grid_spec = pltpu.PrefetchScalarGridSpec(
        num_scalar_prefetch=1,
        grid=(db, n_pages // n),
        in_specs=[page(i) for i in range(n)] + [page(i) for i in range(n)] + [
            pl.BlockSpec((None, n, 8, LANES), lambda b, c, pt: (b, c, 0, 0)),
            pl.BlockSpec((None, 8, LANES), bmap),
            pl.BlockSpec((None, nq, HEAD_DIM), bmap),
            pl.BlockSpec((None, nq, HEAD_DIM), bmap),
            pl.BlockSpec((None, nq, HEAD_DIM), bmap),
            pl.BlockSpec((None, 1, nq), bmap),
            pl.BlockSpec((nq, FLAT), const),
            pl.BlockSpec((nq, nq), const),
            pl.BlockSpec((nq, nq), const),
        ],
        out_specs=pl.BlockSpec((None, nq, HEAD_DIM), bmap),
        scratch_shapes=[pltpu.VMEM((nq, 1), F32), pltpu.VMEM((nq, 1), F32), pltpu.VMEM((nq, HEAD_DIM), F32)],
    )
    del t_new
    return pl.pallas_call(
        _attn_sample_kernel,
        grid_spec=grid_spec,
        out_shape=jax.ShapeDtypeStruct((db, nq, HEAD_DIM), F32),
        compiler_params=_cparams(("arbitrary", "arbitrary")),
        name="attn_sample",
    )(page_table.reshape(-1), *([cache_k] * n), *([cache_v] * n), fpast, ftot, q32, kn32, vn32, lfn,
      jnp.asarray(hmask), jnp.asarray(nmask), jnp.asarray(ncum, dtype=BF16))


def _merge_kernel(ys_ref, ya_ref, x_ref, gate1_ref, shift2_ref, scale2_ref, gs_ref, ga_ref, g2_ref,
                  wglu_ref, bglu_ref, wos_ref, woa_ref, wr_ref, br_ref, cnt_in_ref, strict_ref,
                  x1_ref, h2_ref, eidx_ref, gate_ref, rank_ref, cnt_out_ref, cnt_scr, *, hi_prec):
    first = (pl.program_id(0) == 0) & (pl.program_id(1) == 0)

    @pl.when(first)
    def _():
        cnt_scr[...] = cnt_in_ref[...]

    mm = _dot_x3 if hi_prec else (lambda a, b: _dot(a.astype(BF16), b))
    y = jax.nn.gelu(ys_ref[...])
    y = y * jax.nn.sigmoid(mm(y, wglu_ref[...]) + bglu_ref[...])
    mix = mm(_rms(y, gs_ref[...]), wos_ref[...]) + mm(_rms(ya_ref[...], ga_ref[...]), woa_ref[...])
    x1 = x_ref[...] + gate1_ref[...] * mix
    x1_ref[...] = x1
    h2 = _rms(x1, g2_ref[...]) * (1.0 + scale2_ref[...]) + shift2_ref[...]
    h2_ref[...] = h2
    logits = _dot_x3(h2, wr_ref[...]) + br_ref[...]
    rows = logits.shape[0]
    lane = lax.broadcasted_iota(jnp.int32, (rows, LANES), 1)
    lane_f = lane.astype(F32)
    vals, idxs = [], []
    for _ in range(TOP_K):
        m = jnp.max(logits, axis=1, keepdims=True)
        idx = jnp.min(jnp.where(logits == m, lane_f, float(LANES)), axis=1, keepdims=True)
        vals.append(m)
        idxs.append(idx)
        logits = jnp.where(lane_f == idx, -jnp.inf, logits)
    exps = [jnp.exp(v - vals[0]) for v in vals]
    den = exps[0] + exps[1] + exps[2] + exps[3]
    cnt = cnt_scr[...]
    strict = strict_ref[...]
    e_out = jnp.zeros((rows, LANES), F32)
    g_out = jnp.zeros((rows, LANES), F32)
    r_out = jnp.zeros((rows, LANES), F32)
    for kk in range(TOP_K):
        onehot = lane_f == idxs[kk]
        oh = jnp.where(onehot, 1.0, 0.0)
        before = _dot(strict, oh.astype(BF16)) + cnt
        rank = jnp.sum(jnp.where(onehot, before, 0.0), axis=1, keepdims=True)
        cnt = cnt + jnp.sum(oh, axis=0, keepdims=True)
        e_out = jnp.where(lane == kk, idxs[kk], e_out)
        g_out = jnp.where(lane == kk, exps[kk] / den, g_out)
        r_out = jnp.where(lane == kk, rank, r_out)
    cnt_scr[...] = cnt
    cnt_out_ref[...] = cnt
    eidx_ref[...] = e_out.astype(jnp.int32)
    gate_ref[...] = g_out
    rank_ref[...] = r_out.astype(jnp.int32)


def _merge(ys, ya, x, gate1, shift2, scale2, g_out_ssm, g_out_att, g2, w_glu, b_glu, w_out_s, w_out_a,
           w_router, b_router, cnt_in, *, tile, hi_prec):
    s_dim, r_dim, _ = x.shape
    rm = gate1.shape[1]
    row = lambda s, t: (s, t, 0)
    const2 = lambda s, t: (0, 0)
    mod_map = (lambda s, t: (s, 0, 0)) if rm == 1 else row
    mod_blk = (None, 1, D_MODEL) if rm == 1 else (None, tile, D_MODEL)
    full = lambda a: pl.BlockSpec(a.shape, const2)
    half = pl.BlockSpec((None, tile, SSM_WIDTH), row)
    wide = pl.BlockSpec((None, tile, D_MODEL), row)
    narrow = pl.BlockSpec((None, tile, LANES), row)
    strict = _tri_matrix(tile, strict=True)
    args = (ys, ya, x, gate1, shift2, scale2, g_out_ssm, g_out_att, g2, w_glu, b_glu, w_out_s, w_out_a,
            w_router, b_router, cnt_in, strict)
    return pl.pallas_call(
        functools.partial(_merge_kernel, hi_prec=hi_prec),
        grid=(s_dim, r_dim // tile),
        in_specs=[half, half, wide, pl.BlockSpec(mod_blk, mod_map), pl.BlockSpec(mod_blk, mod_map),
                  pl.BlockSpec(mod_blk, mod_map)] + [full(a) for a in args[6:]],
        out_specs=[wide, wide, narrow, narrow, narrow, pl.BlockSpec((1, LANES), const2)],
        out_shape=[jax.ShapeDtypeStruct((s_dim, r_dim, D_MODEL), F32),
                   jax.ShapeDtypeStruct((s_dim, r_dim, D_MODEL), F32),
                   jax.ShapeDtypeStruct((s_dim, r_dim, LANES), jnp.int32),
                   jax.ShapeDtypeStruct((s_dim, r_dim, LANES), F32),
                   jax.ShapeDtypeStruct((s_dim, r_dim, LANES), jnp.int32),
                   jax.ShapeDtypeStruct((1, LANES), F32)],
        scratch_shapes=[pltpu.VMEM((1, LANES), F32)],
        compiler_params=_cparams(("arbitrary", "arbitrary")),
        name="merge_hi" if hi_prec else "merge",
    )(*args)


MOE_TILE = 256
ROW_TILE = 128


def _dispatch_kernel(fill_ref, dest_ref, hp_ref, hs_ref, xpad_ref, zero_scr, sem):
    i = pl.program_id(0)
    last = pl.num_programs(0) - 1
    rows = hp_ref.shape[0]

    def scatter(h_ref):
        def row_copy(src_row, dst_row):
            return pltpu.make_async_copy(h_ref.at[pl.ds(src_row, 1)], xpad_ref.at[pl.ds(dst_row, 1)], sem)

        def issue(a, carry):
            row_copy(a // TOP_K, dest_ref[0, a]).start()
            return carry

        lax.fori_loop(0, rows * TOP_K, issue, 0)

        def drain(a, carry):
            row_copy(0, 0).wait()
            return carry

        lax.fori_loop(0, rows * TOP_K, drain, 0)

    @pl.when(i < last)
    def _():
        scatter(hp_ref)

    @pl.when(i == last)
    def _():
        scatter(hs_ref)
        zero_scr[...] = jnp.zeros_like(zero_scr)

        def zero_row(dst_row):
            return pltpu.make_async_copy(zero_scr.at[pl.ds(0, 1)], xpad_ref.at[pl.ds(dst_row, 1)], sem)

        def per_expert(e, carry):
            start, count = fill_ref[2 * e], fill_ref[2 * e + 1]

            def fill(a, c2):
                zero_row(start + a).start()
                return c2

            lax.fori_loop(0, count, fill, 0)

            def fill_wait(a, c2):
                zero_row(0).wait()
                return c2

            lax.fori_loop(0, count, fill_wait, 0)
            return carry

        lax.fori_loop(0, N_EXPERTS, per_expert, 0)

        def zero_tile(t):
            return pltpu.make_async_copy(zero_scr, xpad_ref.at[pl.ds(t * MOE_TILE, MOE_TILE)], sem)

        n_used = fill_ref[2 * N_EXPERTS]
        n_tiles = xpad_ref.shape[0] // MOE_TILE

        def tail(t, carry):
            zero_tile(t).start()
            return carry

        lax.fori_loop(n_used, n_tiles, tail, 0)

        def tail_wait(t, carry):
            zero_tile(0).wait()
            return carry

        lax.fori_loop(n_used, n_tiles, tail_wait, 0)


def _dispatch(h2p, h2s, dest, fill, cap):
    ntp = h2p.shape[0] // ROW_TILE
    nt = ntp + 1
    grid_spec = pltpu.PrefetchScalarGridSpec(
        num_scalar_prefetch=1,
        grid=(nt,),
        in_specs=[pl.BlockSpec((None, 1, ROW_TILE * TOP_K), lambda i, f: (i, 0, 0), memory_space=pltpu.SMEM),
                  pl.BlockSpec((ROW_TILE, D_MODEL), lambda i, f: (jnp.minimum(i, ntp - 1), 0)),
                  pl.BlockSpec((ROW_TILE, D_MODEL), lambda i, f: (0, 0))],
        out_specs=pl.BlockSpec(memory_space=pl.ANY),
        scratch_shapes=[pltpu.VMEM((MOE_TILE, D_MODEL), F32), pltpu.SemaphoreType.DMA(())],
    )
    return pl.pallas_call(
        _dispatch_kernel,
        grid_spec=grid_spec,
        out_shape=jax.ShapeDtypeStruct((cap, D_MODEL), F32),
        compiler_params=_cparams(("arbitrary",)),
        name="dispatch",
    )(fill, dest.reshape(nt, 1, ROW_TILE * TOP_K), h2p, h2s)


def _experts_kernel(te_ref, nu_ref, x_ref, wgu_ref, bgu_ref, wd_ref, bd_ref, y_ref, wgu_scr, wd_scr):
    i = pl.program_id(0)
    prev = te_ref[jnp.maximum(i - 1, 0)]
    fresh = (i == 0) | (te_ref[i] != prev)

    @pl.when(fresh)
    def _():
        wgu_scr[...] = wgu_ref[...].astype(BF16)
        wd_scr[...] = wd_ref[...].astype(BF16)

    @pl.when(i < nu_ref[0])
    def _():
        gu = _dot(x_ref[...].astype(BF16), wgu_scr[...]) + bgu_ref[...]
        g = jnp.minimum(gu[:, :D_EXPERT], SWIGLU_LIMIT)
        up = jnp.clip(gu[:, D_EXPERT:], -SWIGLU_LIMIT, SWIGLU_LIMIT)
        act = (up + 1.0) * (g * jax.nn.sigmoid(SWIGLU_ALPHA * g))
        y_ref[...] = _dot(act.astype(BF16), wd_scr[...]) + bd_ref[...]

    @pl.when(i >= nu_ref[0])
    def _():
        y_ref[...] = jnp.zeros_like(y_ref)


def _experts(xpad, tile_expert, n_used, w_gate_up, b_gate_up, w_down, b_down, l):
    cap = xpad.shape[0]
    n_tiles = cap // MOE_TILE
    tmap = lambda i, te, nu: (jnp.minimum(i, nu[0] - 1), 0)
    grid_spec = pltpu.PrefetchScalarGridSpec(
        num_scalar_prefetch=2,
        grid=(n_tiles,),
        in_specs=[
            pl.BlockSpec((MOE_TILE, D_MODEL), tmap),
            pl.BlockSpec((None, None, D_MODEL, 2 * D_EXPERT), lambda i, te, nu: (l, te[i], 0, 0)),
            pl.BlockSpec((None, None, 1, 2 * D_EXPERT), lambda i, te, nu: (l, te[i], 0, 0)),
            pl.BlockSpec((None, None, D_EXPERT, D_MODEL), lambda i, te, nu: (l, te[i], 0, 0)),
            pl.BlockSpec((None, None, 1, D_MODEL), lambda i, te, nu: (l, te[i], 0, 0)),
        ],
        out_specs=pl.BlockSpec((MOE_TILE, D_MODEL), lambda i, te, nu: (i, 0)),
        scratch_shapes=[pltpu.VMEM((D_MODEL, 2 * D_EXPERT), BF16), pltpu.VMEM((D_EXPERT, D_MODEL), BF16)],
    )
    return pl.pallas_call(
        _experts_kernel,
        grid_spec=grid_spec,
        out_shape=jax.ShapeDtypeStruct((cap, D_MODEL), F32),
        compiler_params=_cparams(("arbitrary",)),
        name="experts",
    )(tile_expert, n_used, xpad, w_gate_up, b_gate_up.reshape(DEPTH, N_EXPERTS, 1, 2 * D_EXPERT),
      w_down, b_down.reshape(DEPTH, N_EXPERTS, 1, D_MODEL))


def _combine_kernel(dest_ref, x1_ref, gate2_ref, g_ref, ypad_ref, o_ref, buf, sem):
    rows = x1_ref.shape[0]

    def row_copy(src_row, slot):
        return pltpu.make_async_copy(ypad_ref.at[pl.ds(src_row, 1)], buf.at[pl.ds(slot, 1)], sem)

    def issue(a, carry):
        row_copy(dest_ref[0, a], (a % TOP_K) * rows + a // TOP_K).start()
        return carry

    lax.fori_loop(0, rows * TOP_K, issue, 0)

    def drain(a, carry):
        row_copy(0, 0).wait()
        return carry

    lax.fori_loop(0, rows * TOP_K, drain, 0)
    g = g_ref[...]
    moe = jnp.zeros((rows, D_MODEL), F32)
    for kk in range(TOP_K):
        moe = moe + g[:, kk:kk + 1] * buf[kk * rows:(kk + 1) * rows, :]
    o_ref[...] = x1_ref[...] + gate2_ref[...] * moe


def _combine(x1, gate2, gates, dest, ypad):
    s_dim, r_dim, _ = x1.shape
    rm = gate2.shape[1]
    nt = r_dim // ROW_TILE
    row = lambda s, t: (s, t, 0)
    mod_map = (lambda s, t: (s, 0, 0)) if rm == 1 else row
    mod_blk = (None, 1, D_MODEL) if rm == 1 else (None, ROW_TILE, D_MODEL)
    return pl.pallas_call(
        _combine_kernel,
        grid=(s_dim, nt),
        in_specs=[pl.BlockSpec((None, 1, ROW_TILE * TOP_K), lambda s, t: (s * nt + t, 0, 0),
                               memory_space=pltpu.SMEM),
                  pl.BlockSpec((None, ROW_TILE, D_MODEL), row), pl.BlockSpec(mod_blk, mod_map),
                  pl.BlockSpec((None, ROW_TILE, LANES), row), pl.BlockSpec(memory_space=pl.ANY)],
        out_specs=pl.BlockSpec((None, ROW_TILE, D_MODEL), row),
        out_shape=jax.ShapeDtypeStruct(x1.shape, F32),
        scratch_shapes=[pltpu.VMEM((TOP_K * ROW_TILE, D_MODEL), F32), pltpu.SemaphoreType.DMA(())],
        compiler_params=_cparams(("arbitrary", "arbitrary")),
        name="combine",
    )(dest.reshape(s_dim * nt, 1, ROW_TILE * TOP_K), x1, gate2, gates, ypad)


def _mods(mod_rows):
    return [m for m in jnp.split(mod_rows, N_MOD, axis=-1)]


def kernel(x_prompt, x_sample, c_prompt, c_sample, cache_k, cache_v, cache_logf, state_ssm_re, state_ssm_im,
           page_table, w_ada, b_ada, g_norm1, g_norm2, w_in, b_forget, g_q, g_k, ssm_lam_re, ssm_lam_im,
           ssm_log_dt, ssm_b_re, ssm_b_im, ssm_c_re, ssm_c_im, ssm_d, w_glu, b_glu, g_out_ssm, g_out_att, w_out,
           w_router, b_router, w_gate_up, b_gate_up, w_down, b_down):
    bsz, seq, _ = x_prompt.shape
    db, t_new, _ = x_sample.shape
    n_pool = cache_k.shape[1]
    n_seq = bsz + db
    c_rows = -(-n_seq // 8) * 8
    c_all = jnp.concatenate([c_prompt, c_sample, jnp.zeros((c_rows - n_seq, D_MODEL), F32)], axis=0)
    mod = _adaln(c_all, w_ada, b_ada)
    prep = _ssm_prep(ssm_lam_re, ssm_lam_im, ssm_log_dt, ssm_b_re, ssm_b_im, ssm_c_re, ssm_c_im, ssm_d)
    logf_view = cache_logf.reshape(DEPTH, n_pool, 8, LANES)
    st_re = state_ssm_re.reshape(DEPTH, db, SSM_GROUPS * SSM_STATE)
    st_im = state_ssm_im.reshape(DEPTH, db, SSM_GROUPS * SSM_STATE)
    n_tok_p, n_tok_s = bsz * seq, db * t_new
    assert n_tok_s == ROW_TILE and n_tok_p % ROW_TILE == 0
    n_tiles = -(-(n_tok_p + n_tok_s) * TOP_K // MOE_TILE) + N_EXPERTS
    cap = n_tiles * MOE_TILE
    pad_f = LANES - ATT_HEADS
    pad_e = LANES - N_EXPERTS

    xp = x_prompt
    xs = jnp.swapaxes(x_sample, 0, 1)
    outs = {name: [] for name in ('kp', 'vp', 'lfp', 'srp', 'sip', 'ks', 'vs', 'lfs', 'srs', 'sis')}
    for l in range(DEPTH):
        mp = [m[:, None, :] for m in _mods(mod[l, :bsz])]
        ms = [jnp.broadcast_to(m[None], (t_new, db, D_MODEL)) for m in _mods(mod[l, bsz:n_seq])]
        w_main = w_in[l, :, :IN_MAIN]
        w_f = jnp.pad(w_in[l, :, IN_MAIN:], ((0, 0), (0, pad_f)))
        b_f = jnp.pad(b_forget[l], (0, pad_f))[None]
        gq = jnp.tile(g_q[l], ATT_HEADS)[None]
        gk = jnp.tile(g_k[l], ATT_HEADS)[None]
        g1 = g_norm1[l][None]
        g2 = g_norm2[l][None]
        w_os, w_oa = w_out[l, :SSM_WIDTH], w_out[l, SSM_WIDTH:]
        w_r = jnp.pad(w_router[l], ((0, 0), (0, pad_e)))
        b_r = jnp.pad(b_router[l], (0, pad_e), constant_values=NEG_BIG)[None]
        small = (g_out_ssm[l][None], g_out_att[l][None], g2)

        u, q, k, v, kb, vb, lf, fc = _in_proj(xp, mp[0], mp[1], g1, w_main.astype(BF16), w_f.astype(BF16), b_f,
                                              gq, gk, tile=512, hi_prec=False, with_cumsum=True)
        y_ssm, s_re, s_im = _ssm_prompt(u, prep, l)
        y_att = _attn_prompt(q, kb, vb, fc, jnp.swapaxes(fc[..., :ATT_HEADS], 1, 2))
        x1p, h2p, e_p, gate_p, rank_p, cnt_p = _merge(
            y_ssm, y_att, xp, mp[2], mp[3], mp[4], *small, w_glu[l].astype(BF16), b_glu[l][None],
            w_os.astype(BF16), w_oa.astype(BF16), w_r, b_r, jnp.zeros((1, LANES), F32), tile=512, hi_prec=False)
        outs['kp'].append(k.reshape(bsz, seq, ATT_HEADS, HEAD_DIM))
        outs['vp'].append(v.reshape(bsz, seq, ATT_HEADS, HEAD_DIM))
        outs['lfp'].append(lf[..., :ATT_HEADS])
        outs['srp'].append(s_re)
        outs['sip'].append(s_im)

        u_s, q_s, k_s, v_s, _, _, lf_s, _ = _in_proj(xs, ms[0], ms[1], g1, w_main, w_f, b_f, gq, gk,
                                                     tile=db, hi_prec=True, with_cumsum=False)
        y_ssm_s, o_re, o_im = _ssm_sample(u_s, st_re, st_im, prep, l)
        per_seq = lambda a: jnp.swapaxes(a, 0, 1).reshape(db, t_new * ATT_HEADS, HEAD_DIM)
        lfn = jnp.swapaxes(lf_s[..., :ATT_HEADS], 0, 1).reshape(db, 1, t_new * ATT_HEADS)
        fpast, ftot = _fpast(logf_view, page_table, l)
        o_s = _attn_sample(per_seq(q_s), per_seq(k_s), per_seq(v_s), lfn, cache_k, cache_v, fpast, ftot,
                           page_table, l)
        y_att_s = jnp.swapaxes(o_s.reshape(db, t_new, ATT_WIDTH), 0, 1)
        x1s, h2s, e_s, gate_s, rank_s, cnt = _merge(
            y_ssm_s, y_att_s, xs, ms[2], ms[3], ms[4], *small, w_glu[l], b_glu[l][None], w_os, w_oa, w_r, b_r,
            cnt_p, tile=db, hi_prec=True)
        outs['ks'].append(jnp.swapaxes(k_s, 0, 1).reshape(db, t_new, ATT_HEADS, HEAD_DIM))
        outs['vs'].append(jnp.swapaxes(v_s, 0, 1).reshape(db, t_new, ATT_HEADS, HEAD_DIM))
        outs['lfs'].append(jnp.swapaxes(lf_s[..., :ATT_HEADS], 0, 1))
        outs['srs'].append(o_re.reshape(db, SSM_GROUPS, SSM_STATE))
        outs['sis'].append(o_im.reshape(db, SSM_GROUPS, SSM_STATE))

        counts = cnt[0, :N_EXPERTS].astype(jnp.int32)
        padded = (counts + MOE_TILE - 1) // MOE_TILE * MOE_TILE
        pend = jnp.cumsum(padded)
        pstart = pend - padded
        dest_p = (pstart[e_p[..., :TOP_K]] + rank_p[..., :TOP_K]).reshape(-1)
        dest_s = (pstart[e_s[..., :TOP_K]] + rank_s[..., :TOP_K]).reshape(-1)
        tile_start = jnp.arange(n_tiles, dtype=jnp.int32) * MOE_TILE
        tile_expert = jnp.minimum(jnp.searchsorted(pend, tile_start, side='right'), N_EXPERTS - 1).astype(jnp.int32)
        n_used = (pend[-1:] // MOE_TILE).astype(jnp.int32)
        fill = jnp.stack([pstart + counts, padded - counts], axis=-1).reshape(-1)
        fill = jnp.concatenate([fill, n_used]).astype(jnp.int32)
        xpad = _dispatch(h2p.reshape(n_tok_p, D_MODEL), h2s.reshape(n_tok_s, D_MODEL),
                         jnp.concatenate([dest_p, dest_s]), fill, cap)
        ypad = _experts(xpad, tile_expert, n_used, w_gate_up, b_gate_up, w_down, b_down, l)
        xp = _combine(x1p, mp[5], gate_p, dest_p, ypad)
        xs = _combine(x1s.reshape(1, n_tok_s, D_MODEL), ms[5].reshape(1, n_tok_s, D_MODEL),
                      gate_s.reshape(1, n_tok_s, LANES), dest_s, ypad).reshape(t_new, db, D_MODEL)

    st = lambda name: jnp.stack(outs[name])
    return (xp, jnp.swapaxes(xs, 0, 1), st('kp'), st('vp'), st('lfp'), st('srp'), st('sip'),
            st('ks'), st('vs'), st('lfs'), st('srs'), st('sis'))
```

```python
import functools
import math

import jax
import jax.numpy as jnp
import numpy as np
from jax import lax
from jax.experimental import pallas as pl
from jax.experimental.pallas import tpu as pltpu

F32 = jnp.float32
BF16 = jnp.bfloat16

D_MODEL = 1024
DEPTH = 4
PAGE_SIZE = 128
SSM_WIDTH = 512
SSM_GROUP_CH = 16
SSM_GROUPS = 32
SSM_STATE = 64
ATT_WIDTH = 512
HEAD_DIM = 64
ATT_HEADS = 8
ATT_SCALE = HEAD_DIM ** -0.5
IN_MAIN = SSM_WIDTH + 3 * ATT_WIDTH
N_EXPERTS = 32
TOP_K = 4
D_EXPERT = 1024
SWIGLU_LIMIT = 7.0
SWIGLU_ALPHA = 1.702
N_MOD = 6
NORM_EPS = 1e-6

LANES = 128
SSM_CHUNK = 8
SSM_BLOCKS = SSM_WIDTH // LANES
GROUPS_PER_BLOCK = LANES // SSM_GROUP_CH
STATES_PER_BLOCK = GROUPS_PER_BLOCK * SSM_STATE
VMEM_LIMIT = 56 * 1024 * 1024


def _cparams(sem):
    return pltpu.CompilerParams(dimension_semantics=sem, vmem_limit_bytes=VMEM_LIMIT)


def _split2(x):
    hi = x.astype(BF16)
    lo = (x - hi.astype(F32)).astype(BF16)
    return hi, lo


def _split3(x):
    hi = x.astype(BF16)
    r = x - hi.astype(F32)
    mid = r.astype(BF16)
    lo = (r - mid.astype(F32)).astype(BF16)
    return hi, mid, lo


def _dot(a, b):
    return jnp.dot(a, b, preferred_element_type=F32)


def _dot_x3(a, b):
    ah, al = _split2(a)
    bh, bl = _split2(b)
    return _dot(ah, bh) + (_dot(ah, bl) + _dot(al, bh))


def _dot_nt(a, b):
    return lax.dot_general(a, b, (((1,), (1,)), ((), ())), preferred_element_type=F32)


def _rms(x, g):
    return x * lax.rsqrt(jnp.mean(x * x, axis=-1, keepdims=True) + NORM_EPS) * g


def _adaln_kernel(c_ref, w_ref, b_ref, o_ref):
    c = c_ref[...]
    a = c * jax.nn.sigmoid(c)
    o_ref[...] = _dot_x3(a, w_ref[...]) + b_ref[...]


def _adaln(c_all, w_ada, b_ada):
    rows = c_all.shape[0]
    tn = 1536
    nt = (N_MOD * D_MODEL) // tn
    return pl.pallas_call(
        _adaln_kernel,
        grid=(DEPTH, nt),
        in_specs=[
            pl.BlockSpec((rows, D_MODEL), lambda l, n: (0, 0)),
            pl.BlockSpec((None, D_MODEL, tn), lambda l, n: (l, 0, n)),
            pl.BlockSpec((None, 1, tn), lambda l, n: (l, 0, n)),
        ],
        out_specs=pl.BlockSpec((None, rows, tn), lambda l, n: (l, 0, n)),
        out_shape=jax.ShapeDtypeStruct((DEPTH, rows, N_MOD * D_MODEL), F32),
        compiler_params=_cparams(("arbitrary", "arbitrary")),
        name="adaln",
    )(c_all, w_ada, b_ada.reshape(DEPTH, 1, N_MOD * D_MODEL))


def _log_sigmoid(x):
    return jnp.minimum(x, 0.0) - jnp.log1p(jnp.exp(-jnp.abs(x)))


def _inproj_kernel(x_ref, shift_ref, scale_ref, g1_ref, wm_ref, wf_ref, bf_ref, gq_ref, gk_ref,
                   seg_ref, tri_ref, place_ref, kone_ref,
                   u_ref, q_ref, k_ref, v_ref, kcat_ref, vcat_ref, lf_ref, fc_ref, carry_ref,
                   *, hi_prec, with_cumsum):
    x = x_ref[...]
    h = _rms(x, g1_ref[...]) * (1.0 + scale_ref[...]) + shift_ref[...]
    if hi_prec:
        z = _dot_x3(h, wm_ref[...])
        zf = _dot_x3(h, wf_ref[...])
    else:
        hb = h.astype(BF16)
        z = _dot(hb, wm_ref[...])
        zf = _dot(hb, wf_ref[...])
    u_ref[...] = z[:, :SSM_WIDTH]
    q = z[:, SSM_WIDTH:SSM_WIDTH + ATT_WIDTH]
    k = z[:, SSM_WIDTH + ATT_WIDTH:SSM_WIDTH + 2 * ATT_WIDTH]
    v = z[:, SSM_WIDTH + 2 * ATT_WIDTH:]
    seg = seg_ref[...]

    def head_norm(t, g):
        hi, lo = _split2(t * t)
        ms = _dot(hi, seg) + _dot(lo, seg)
        return t * lax.rsqrt(ms + NORM_EPS) * g

    qn = head_norm(q, gq_ref[...])
    kn = head_norm(k, gk_ref[...])
    q_ref[...] = (qn * ATT_SCALE).astype(BF16)
    k_ref[...] = kn
    v_ref[...] = v
    logf = _log_sigmoid(zf + bf_ref[...])
    lf_ref[...] = logf
    if with_cumsum:
        @pl.when(pl.program_id(1) == 0)
        def _():
            carry_ref[...] = jnp.zeros_like(carry_ref)

        tri = tri_ref[...]
        hi, mid, lo = _split3(logf)
        fc = carry_ref[...] + (_dot(tri, hi) + (_dot(tri, mid) + _dot(tri, lo)))
        fc_ref[...] = fc
        carry_ref[...] = fc[-1:, :]
        hi, mid, lo = _split3(fc)
        kext = (_dot(hi, place_ref[0]) + (_dot(mid, place_ref[1]) + _dot(lo, place_ref[2]))
                + kone_ref[...]).astype(BF16)
        kb = kn.astype(BF16)
        vb = v.astype(BF16)
        lane = lax.broadcasted_iota(jnp.int32, (x.shape[0], LANES), 1)
        vext = jnp.where(lane == 0, 1.0, 0.0).astype(BF16)
        for p in range(ATT_HEADS // 2):
            lo_l, hi_l = p * LANES, (p + 1) * LANES
            kcat_ref[:, 2 * lo_l:2 * lo_l + LANES] = kb[:, lo_l:hi_l]
            kcat_ref[:, 2 * lo_l + LANES:2 * hi_l] = kext[:, lo_l:hi_l]
            vcat_ref[:, 2 * lo_l:2 * lo_l + LANES] = vb[:, lo_l:hi_l]
            vcat_ref[:, 2 * lo_l + LANES:2 * hi_l] = vext
    else:
        fc_ref[...] = logf
        kcat_ref[...] = jnp.zeros_like(kcat_ref)
        vcat_ref[...] = jnp.zeros_like(vcat_ref)


def _seg_matrix():
    i = np.arange(ATT_WIDTH) // HEAD_DIM
    return jnp.asarray((i[:, None] == i[None, :]).astype(np.float32) / HEAD_DIM, dtype=BF16)


def _tri_matrix(n, strict=False):
    i = np.arange(n)
    m = (i[None, :] < i[:, None]) if strict else (i[None, :] <= i[:, None])
    return jnp.asarray(m.astype(np.float32), dtype=BF16)


N_BIAS = 3


def _bias_placement():
    place = np.zeros((N_BIAS, LANES, ATT_WIDTH), np.float32)
    kone = np.zeros((1, ATT_WIDTH), np.float32)
    for h in range(ATT_HEADS):
        base = (h // 2) * LANES
        for part in range(N_BIAS):
            place[part, h, base + N_BIAS * (1 + h % 2) + part] = -1.0
    for p in range(ATT_HEADS // 2):
        kone[0, p * LANES:p * LANES + N_BIAS] = 1.0
    return jnp.asarray(place, dtype=BF16), jnp.asarray(kone)


def _in_proj(x, shift, scale, g1, w_main, w_f, b_f, g_q, g_k, *, tile, hi_prec, with_cumsum):
    s_dim, r_dim, _ = x.shape
    rm = shift.shape[1]
    nt = r_dim // tile
    row = lambda s, t: (s, t, 0)
    const2 = lambda s, t: (0, 0)
    mod_map = (lambda s, t: (s, 0, 0)) if rm == 1 else row
    mod_blk = (None, 1, D_MODEL) if rm == 1 else (None, tile, D_MODEL)
    outs = [
        jax.ShapeDtypeStruct((s_dim, r_dim, SSM_WIDTH), F32),
        jax.ShapeDtypeStruct((s_dim, r_dim, ATT_WIDTH), BF16),
        jax.ShapeDtypeStruct((s_dim, r_dim, ATT_WIDTH), F32),
        jax.ShapeDtypeStruct((s_dim, r_dim, ATT_WIDTH), F32),
        jax.ShapeDtypeStruct((s_dim, r_dim, 2 * ATT_WIDTH), BF16),
        jax.ShapeDtypeStruct((s_dim, r_dim, 2 * ATT_WIDTH), BF16),
        jax.ShapeDtypeStruct((s_dim, r_dim, LANES), F32),
        jax.ShapeDtypeStruct((s_dim, r_dim, LANES), F32),
    ]
    wide = pl.BlockSpec((None, tile, ATT_WIDTH), row)
    cat = pl.BlockSpec((None, tile, 2 * ATT_WIDTH), row)
    narrow = pl.BlockSpec((None, tile, LANES), row)
    place, kone = _bias_placement()
    return pl.pallas_call(
        functools.partial(_inproj_kernel, hi_prec=hi_prec, with_cumsum=with_cumsum),
        grid=(s_dim, nt),
        in_specs=[
            pl.BlockSpec((None, tile, D_MODEL), row),
            pl.BlockSpec(mod_blk, mod_map),
            pl.BlockSpec(mod_blk, mod_map),
            pl.BlockSpec((1, D_MODEL), const2),
            pl.BlockSpec(w_main.shape, const2),
            pl.BlockSpec(w_f.shape, const2),
            pl.BlockSpec((1, LANES), const2),
            pl.BlockSpec((1, ATT_WIDTH), const2),
            pl.BlockSpec((1, ATT_WIDTH), const2),
            pl.BlockSpec((ATT_WIDTH, ATT_WIDTH), const2),
            pl.BlockSpec((tile, tile), const2),
            pl.BlockSpec(place.shape, lambda s, t: (0, 0, 0)),
            pl.BlockSpec(kone.shape, const2),
        ],
        out_specs=[wide, wide, wide, wide, cat, cat, narrow, narrow],
        out_shape=outs,
        scratch_shapes=[pltpu.VMEM((1, LANES), F32)],
        compiler_params=_cparams(("arbitrary", "arbitrary")),
        name="in_proj_hi" if hi_prec else "in_proj",
    )(x, shift, scale, g1, w_main, w_f, b_f, g_q, g_k, _seg_matrix(), _tri_matrix(tile), place, kone)


def _discretize(lam_re, lam_im, log_dt):
    dt = jnp.exp(log_dt)
    mag = jnp.exp(lam_re * dt)
    ang = lam_im * dt
    a_re = mag * jnp.cos(ang)
    a_im = mag * jnp.sin(ang)
    den = lam_re * lam_re + lam_im * lam_im
    nr = a_re - 1.0
    coef_re = (nr * lam_re + a_im * lam_im) / den
    coef_im = (a_im * lam_re - nr * lam_im) / den
    return a_re, a_im, coef_re, coef_im


def _powers(a_re, a_im, n):
    pr, pi = [jnp.ones_like(a_re)], [jnp.zeros_like(a_im)]
    for _ in range(n):
        r, i = pr[-1], pi[-1]
        pr.append(r * a_re - i * a_im)
        pi.append(r * a_im + i * a_re)
    return pr, pi


def _ssm_prep_kernel(lr_row, li_row, dt_row, lr_col, li_col, dt_col, btr_ref, bti_ref, ctr_ref, cti_ref,
                     d_ref, tmat_ref, wz_ref, wc_ref, apow_ref, bbt_ref):
    L, W = SSM_CHUNK, LANES
    ar, ai, cr, ci = _discretize(lr_row[...], li_row[...], dt_row[...])
    acr, aci, _, _ = _discretize(lr_col[...], li_col[...], dt_col[...])
    btr, bti = btr_ref[...], bti_ref[...]
    bb_re = cr * btr - ci * bti
    bb_im = cr * bti + ci * btr
    bbt_ref[0] = bb_re
    bbt_ref[1] = bb_im
    ctr, cti = ctr_ref[...], cti_ref[...]
    pr, pi = _powers(ar, ai, L)
    pcr, pci = _powers(acr, aci, L)
    apow_ref[...] = jnp.concatenate(
        [ar, ai, pr[L], pi[L], jnp.zeros((4, STATES_PER_BLOCK), F32)], axis=0)
    eye = (lax.broadcasted_iota(jnp.int32, (W, W), 0) == lax.broadcasted_iota(jnp.int32, (W, W), 1))
    kts = []
    for tau in range(L):
        m_re = pr[tau] * bb_re - pi[tau] * bb_im
        m_im = pr[tau] * bb_im + pi[tau] * bb_re
        kt = _dot_x3(m_re, ctr) - _dot_x3(m_im, cti)
        if tau == 0:
            kt = kt + jnp.where(eye, d_ref[...], 0.0)
        kts.append(kt.astype(BF16))
        jp = L - 1 - tau
        wz_ref[jp * W:(jp + 1) * W, 0:STATES_PER_BLOCK] = m_re.astype(BF16)
        wz_ref[jp * W:(jp + 1) * W, STATES_PER_BLOCK:] = m_im.astype(BF16)
    zero = jnp.zeros((W, W), BF16)
    for jp in range(L):
        for j in range(L):
            tmat_ref[jp * W:(jp + 1) * W, j * W:(j + 1) * W] = kts[j - jp] if j >= jp else zero
    for j in range(L):
        qr, qi = pcr[j + 1], pci[j + 1]
        wc_ref[0:STATES_PER_BLOCK, j * W:(j + 1) * W] = (ctr * qr - cti * qi).astype(BF16)
        wc_ref[STATES_PER_BLOCK:, j * W:(j + 1) * W] = (-(ctr * qi + cti * qr)).astype(BF16)


def _blockdiag(t):
    a, b = t.shape[-2:]
    t = t.reshape(DEPTH, SSM_BLOCKS, GROUPS_PER_BLOCK, a, b)
    eye = jnp.eye(GROUPS_PER_BLOCK, dtype=t.dtype)
    out = jnp.einsum('lkgab,gh->lkgahb', t, eye)
    return out.reshape(DEPTH, SSM_BLOCKS, GROUPS_PER_BLOCK * a, GROUPS_PER_BLOCK * b)


def _ssm_prep(lam_re, lam_im, log_dt, b_re, b_im, c_re, c_im, d):
    S, KW = STATES_PER_BLOCK, SSM_CHUNK * LANES
    ldt = jnp.broadcast_to(log_dt[:, :, None], lam_re.shape)
    rows = [t.reshape(DEPTH, SSM_BLOCKS, 1, S) for t in (lam_re, lam_im, ldt)]
    cols = [t.reshape(DEPTH, SSM_BLOCKS, S, 1) for t in (lam_re, lam_im, ldt)]
    bt = [_blockdiag(jnp.swapaxes(t, -1, -2)) for t in (b_re, b_im)]
    ct = [_blockdiag(jnp.swapaxes(t, -1, -2)) for t in (c_re, c_im)]
    d4 = d.reshape(DEPTH, SSM_BLOCKS, 1, LANES)
    blk = lambda *s: pl.BlockSpec((None, None) + s, lambda l, k: (l, k) + (0,) * len(s))
    big = jax.ShapeDtypeStruct((DEPTH, SSM_BLOCKS, KW, KW), BF16)
    outs = pl.pallas_call(
        _ssm_prep_kernel,
        grid=(DEPTH, SSM_BLOCKS),
        in_specs=[blk(1, S)] * 3 + [blk(S, 1)] * 3 + [blk(LANES, S)] * 2 + [blk(S, LANES)] * 2
                 + [blk(1, LANES)],
        out_specs=[blk(KW, KW), blk(KW, KW), blk(KW, KW), blk(8, S), blk(2, LANES, S)],
        out_shape=[big, big, big,
                   jax.ShapeDtypeStruct((DEPTH, SSM_BLOCKS, 8, S), F32),
                   jax.ShapeDtypeStruct((DEPTH, SSM_BLOCKS, 2, LANES, S), F32)],
        compiler_params=_cparams(("arbitrary", "arbitrary")),
        name="ssm_prep",
    )(*rows, *cols, *bt, *ct, d4)
    tmat, wz, wc, apow, bbt = outs
    return dict(tmat=tmat, wz=wz, wc=wc, apow=apow, bbt=bbt, ct_re=ct[0], ct_im=ct[1], d4=d4)


SSM_SEQ_TILE = 2048
SSM_ROWS = SSM_SEQ_TILE // SSM_CHUNK


def _ssm_prompt_kernel(u_ref, tmat_ref, wz_ref, wc_ref, apow_ref, y_ref, st_ref,
                       up_scr, z_scr, h_scr, carry_scr):
    L, W, S, R = SSM_CHUNK, LANES, STATES_PER_BLOCK, SSM_ROWS

    @pl.when(pl.program_id(2) == 0)
    def _():
        carry_scr[...] = jnp.zeros_like(carry_scr)

    for j in range(L):
        up_scr[:, j * W:(j + 1) * W] = u_ref[pl.ds(j, R, stride=L), :].astype(BF16)
    up = up_scr[...]
    z_scr[...] = _dot(up, wz_ref[...])
    a8r = apow_ref[2:3, :]
    a8i = apow_ref[3:4, :]

    def step(m, carry):
        hr, hi = carry
        h_scr[pl.ds(m, 1), 0:S] = hr
        h_scr[pl.ds(m, 1), S:] = hi
        zr = z_scr[pl.ds(m, 1), 0:S]
        zi = z_scr[pl.ds(m, 1), S:]
        return a8r * hr - a8i * hi + zr, a8r * hi + a8i * hr + zi

    hr, hi = lax.fori_loop(0, R, step, (carry_scr[0:1, :], carry_scr[1:2, :]))
    carry_scr[0:1, :] = hr
    carry_scr[1:2, :] = hi
    st_ref[...] = jnp.concatenate([hr, hi, jnp.zeros((6, S), F32)], axis=0)
    y = _dot(up, tmat_ref[...]) + _dot(h_scr[...].astype(BF16), wc_ref[...])
    for j in range(L):
        y_ref[pl.ds(j, R, stride=L), :] = y[:, j * W:(j + 1) * W]


def _ssm_prompt(u, prep, l):
    bsz, seq, _ = u.shape
    KW, S = SSM_CHUNK * LANES, STATES_PER_BLOCK
    wspec = pl.BlockSpec((None, None, KW, KW), lambda k, b, s: (l, k, 0, 0))
    y, st = pl.pallas_call(
        _ssm_prompt_kernel,
        grid=(SSM_BLOCKS, bsz, seq // SSM_SEQ_TILE),
        in_specs=[
            pl.BlockSpec((None, SSM_SEQ_TILE, LANES), lambda k, b, s: (b, s, k)),
            wspec, wspec, wspec,
            pl.BlockSpec((None, None, 8, S), lambda k, b, s: (l, k, 0, 0)),
        ],
        out_specs=[
            pl.BlockSpec((None, SSM_SEQ_TILE, LANES), lambda k, b, s: (b, s, k)),
            pl.BlockSpec((None, None, 8, S), lambda k, b, s: (b, k, 0, 0)),
        ],
        out_shape=[jax.ShapeDtypeStruct((bsz, seq, SSM_WIDTH), F32),
                   jax.ShapeDtypeStruct((bsz, SSM_BLOCKS, 8, S), F32)],
        scratch_shapes=[pltpu.VMEM((SSM_ROWS, KW), BF16), pltpu.VMEM((SSM_ROWS, KW), F32),
                        pltpu.VMEM((SSM_ROWS, KW), F32), pltpu.VMEM((2, S), F32)],
        compiler_params=_cparams(("arbitrary", "arbitrary", "arbitrary")),
        name="ssm_prompt",
    )(u, prep['tmat'], prep['wz'], prep['wc'], prep['apow'])
    s_re = st[:, :, 0, :].reshape(bsz, SSM_GROUPS, SSM_STATE)
    s_im = st[:, :, 1, :].reshape(bsz, SSM_GROUPS, SSM_STATE)
    return y, s_re, s_im


def _ssm_sample_kernel(u_ref, sre_ref, sim_ref, apow_ref, bbt_ref, ctr_ref, cti_ref, d_ref,
                       y_ref, ore_ref, oim_ref):
    a_re = apow_ref[0:1, :]
    a_im = apow_ref[1:2, :]
    bb_re, bb_im = bbt_ref[0], bbt_ref[1]
    ctr, cti = ctr_ref[...], cti_ref[...]
    s_re, s_im = sre_ref[...], sim_ref[...]
    for j in range(u_ref.shape[0]):
        u = u_ref[j]
        n_re = a_re * s_re - a_im * s_im + _dot_x3(u, bb_re)
        n_im = a_re * s_im + a_im * s_re + _dot_x3(u, bb_im)
        s_re, s_im = n_re, n_im
        y_ref[j] = _dot_x3(s_re, ctr) - _dot_x3(s_im, cti) + d_ref[...] * u
    ore_ref[...] = s_re
    oim_ref[...] = s_im


def _ssm_sample(u, state_re, state_im, prep, l):
    t, db, _ = u.shape
    S = STATES_PER_BLOCK
    lk = lambda *s: pl.BlockSpec((None, None) + s, lambda k: (l, k) + (0,) * len(s))
    st_in = pl.BlockSpec((None, db, S), lambda k: (l, 0, k))
    st_out = pl.BlockSpec((db, S), lambda k: (0, k))
    useq = pl.BlockSpec((t, db, LANES), lambda k: (0, 0, k))
    return pl.pallas_call(
        _ssm_sample_kernel,
        grid=(SSM_BLOCKS,),
        in_specs=[useq, st_in, st_in, lk(8, S), lk(2, LANES, S), lk(S, LANES), lk(S, LANES), lk(1, LANES)],
        out_specs=[useq, st_out, st_out],
        out_shape=[jax.ShapeDtypeStruct((t, db, SSM_WIDTH), F32),
                   jax.ShapeDtypeStruct((db, SSM_GROUPS * SSM_STATE), F32),
                   jax.ShapeDtypeStruct((db, SSM_GROUPS * SSM_STATE), F32)],
        compiler_params=_cparams(("arbitrary",)),
        name="ssm_sample",
    )(u, state_re, state_im, prep['apow'], prep['bbt'], prep['ct_re'], prep['ct_im'], prep['d4'])


ATT_TILE = 512
SHIFT_LIMIT = 30.0
NEG_BIG = -1e30


def _attn_prompt_kernel(qi_ref, kj_ref, q_ref, k_ref, v_ref, fq_ref, fk_ref, o_ref,
                        qm_scr, fq_scr, m_scr, l_scr, acc_scr):
    pair = pl.program_id(1)
    step = pl.program_id(2)
    i = qi_ref[step]
    j = kj_ref[step]
    tq = q_ref.shape[0]
    lane = lax.broadcasted_iota(jnp.int32, (tq, LANES), 1)

    @pl.when(j == 0)
    def _():
        q = q_ref[...]
        fq = fq_ref[...]
        for hh in range(2):
            in_head = (lane >= HEAD_DIM * hh) & (lane < HEAD_DIM * (hh + 1))
            qm_scr[hh] = jnp.where(in_head, q, jnp.zeros_like(q))
            fq_scr[hh] = jnp.sum(jnp.where(lane == 2 * pair + hh, fq, 0.0), axis=1, keepdims=True)
            m_scr[hh] = jnp.full((tq, 1), NEG_BIG, F32)
            l_scr[hh] = jnp.zeros((tq, 1), F32)
            acc_scr[hh] = jnp.zeros((tq, LANES), F32)

    k = k_ref[...]
    v = v_ref[...]
    row = lax.broadcasted_iota(jnp.int32, (tq, tq), 0)
    col = lax.broadcasted_iota(jnp.int32, (tq, tq), 1)
    visible = (col <= row) | (j < i)
    for hh in range(2):
        fk = fk_ref[pl.ds(2 * pair + hh, 1), :]
        s = _dot_nt(qm_scr[hh], k) + (fq_scr[hh] - fk)
        s = jnp.where(visible, s, NEG_BIG)
        m_prev = m_scr[hh]
        m_new = jnp.maximum(m_prev, jnp.max(s, axis=1, keepdims=True))
        alpha = jnp.exp(m_prev - m_new)
        p = jnp.exp(s - m_new)
        l_scr[hh] = alpha * l_scr[hh] + jnp.sum(p, axis=1, keepdims=True)
        acc_scr[hh] = alpha * acc_scr[hh] + _dot(p.astype(BF16), v)
        m_scr[hh] = m_new

    @pl.when(j == i)
    def _():
        o_ref[...] = jnp.where(lane < HEAD_DIM, acc_scr[0] / l_scr[0], acc_scr[1] / l_scr[1])


def _tile_schedule(n):
    qi = np.concatenate([np.full(i + 1, i, np.int32) for i in range(n)])
    kj = np.concatenate([np.arange(i + 1, dtype=np.int32) for i in range(n)])
    return jnp.asarray(qi), jnp.asarray(kj)


def _attn_bounded_kernel(qi_ref, kj_ref, q_ref, k_ref, v_ref, fq_ref, o_ref, qa_scr, acc_scr):
    pair = pl.program_id(1)
    step = pl.program_id(2)
    i = qi_ref[step]
    j = kj_ref[step]
    tq = q_ref.shape[0]
    lane = lax.broadcasted_iota(jnp.int32, (tq, LANES), 1)

    @pl.when(j == 0)
    def _():
        q = q_ref[...]
        fq = fq_ref[...]
        for hh in range(2):
            in_head = (lane >= HEAD_DIM * hh) & (lane < HEAD_DIM * (hh + 1))
            qa_scr[hh, :, 0:LANES] = jnp.where(in_head, q, jnp.zeros_like(q))
            c = jnp.sum(jnp.where(lane == 2 * pair + hh, fq, 0.0), axis=1, keepdims=True)
            c_hi, c_mid, c_lo = [t.astype(F32) for t in _split3(c)]
            ones_at = (lane >= N_BIAS * (1 + hh)) & (lane < N_BIAS * (2 + hh))
            ext = jnp.where(lane == 0, c_hi, jnp.where(lane == 1, c_mid, jnp.where(lane == 2, c_lo,
                            jnp.where(ones_at, 1.0, 0.0))))
            qa_scr[hh, :, LANES:] = ext.astype(BF16)
            acc_scr[hh] = jnp.zeros(acc_scr.shape[1:], F32)

    k = k_ref[...]
    v = v_ref[...]
    row = lax.broadcasted_iota(jnp.int32, (tq, tq), 0)
    col = lax.broadcasted_iota(jnp.int32, (tq, tq), 1)
    visible = (col <= row) | (j < i)
    for hh in range(2):
        s = _dot_nt(qa_scr[hh], k)
        p = jnp.exp(jnp.where(visible, s, NEG_BIG)).astype(BF16)
        acc_scr[hh] += _dot(p, v)

    @pl.when(j == i)
    def _():
        a0, a1 = acc_scr[0], acc_scr[1]
        o_ref[...] = jnp.where(lane < HEAD_DIM, a0[:, :LANES] / a0[:, LANES:LANES + 1],
                               a1[:, :LANES] / a1[:, LANES:LANES + 1])


def _attn_bounded(q, kcat, vcat, fq_shifted):
    bsz, seq, _ = q.shape
    t = ATT_TILE
    qi, kj = _tile_schedule(seq // t)
    qmap = lambda b, p, s, qi, kj: (b, qi[s], p)
    kmap = lambda b, p, s, qi, kj: (b, kj[s], p)
    grid_spec = pltpu.PrefetchScalarGridSpec(
        num_scalar_prefetch=2,
        grid=(bsz, ATT_HEADS // 2, qi.shape[0]),
        in_specs=[
            pl.BlockSpec((None, t, LANES), qmap),
            pl.BlockSpec((None, t, 2 * LANES), kmap),
            pl.BlockSpec((None, t, 2 * LANES), kmap),
            pl.BlockSpec((None, t, LANES), lambda b, p, s, qi, kj: (b, qi[s], 0)),
        ],
        out_specs=pl.BlockSpec((None, t, LANES), qmap),
        scratch_shapes=[pltpu.VMEM((2, t, 2 * LANES), BF16), pltpu.VMEM((2, t, 2 * LANES), F32)],
    )
    return pl.pallas_call(
        _attn_bounded_kernel,
        grid_spec=grid_spec,
        out_shape=jax.ShapeDtypeStruct((bsz, seq, ATT_WIDTH), F32),
        compiler_params=_cparams(("arbitrary", "arbitrary", "arbitrary")),
        name="attn_bounded",
    )(qi, kj, q, kcat, vcat, fq_shifted)


def _attn_prompt(q, kcat, vcat, fcum, fcum_t):
    bsz, seq, _ = q.shape
    t = ATT_TILE
    qi, kj = _tile_schedule(seq // t)
    qmap = lambda b, p, s, qi, kj: (b, qi[s], p)
    kmap = lambda b, p, s, qi, kj: (b, kj[s], 2 * p)
    grid_spec = pltpu.PrefetchScalarGridSpec(
        num_scalar_prefetch=2,
        grid=(bsz, ATT_HEADS // 2, qi.shape[0]),
        in_specs=[
            pl.BlockSpec((None, t, LANES), qmap),
            pl.BlockSpec((None, t, LANES), kmap),
            pl.BlockSpec((None, t, LANES), kmap),
            pl.BlockSpec((None, t, LANES), lambda b, p, s, qi, kj: (b, qi[s], 0)),
            pl.BlockSpec((None, ATT_HEADS, t), lambda b, p, s, qi, kj: (b, 0, kj[s])),
        ],
        out_specs=pl.BlockSpec((None, t, LANES), qmap),
        scratch_shapes=[pltpu.VMEM((2, t, LANES), BF16), pltpu.VMEM((2, t, 1), F32),
                        pltpu.VMEM((2, t, 1), F32), pltpu.VMEM((2, t, 1), F32),
                        pltpu.VMEM((2, t, LANES), F32)],
    )
    return pl.pallas_call(
        _attn_prompt_kernel,
        grid_spec=grid_spec,
        out_shape=jax.ShapeDtypeStruct((bsz, seq, ATT_WIDTH), F32),
        compiler_params=_cparams(("arbitrary", "arbitrary", "arbitrary")),
        name="attn_prompt",
    )(qi, kj, q, kcat, vcat, fcum, fcum_t)


PAGES_PER_STEP = 8
FLAT = PAGE_SIZE * ATT_HEADS


FPAST_ROWS = 512
FPAST_POOL = 512


def _fpast_mats(n_pages):
    src = np.arange(FLAT)
    dst = np.arange(FLAT)
    same = (src[:, None] % ATT_HEADS) == (dst[None, :] // PAGE_SIZE)
    within = same & ((src[:, None] // ATT_HEADS) <= (dst[None, :] % PAGE_SIZE))
    r = np.arange(FPAST_ROWS)
    before = ((r[:, None] // n_pages) == (r[None, :] // n_pages)) & ((r[None, :] % n_pages) < (r[:, None] % n_pages))
    to_bf = lambda m: jnp.asarray(np.asarray(m, np.float32), dtype=BF16)
    return to_bf(within), to_bf(same), to_bf(before)


def _fpast_kernel(pt_ref, lf_ref, within_ref, same_ref, before_ref, f_ref, x_scr):
    c = pl.program_id(1)

    @pl.when(c == 0)
    def _():
        x_scr[...] = jnp.zeros_like(x_scr)

    def times(parts, m):
        return _dot(parts[0], m) + (_dot(parts[1], m) + _dot(parts[2], m))

    pool = lax.broadcasted_iota(jnp.int32, (FPAST_ROWS, FPAST_POOL), 1) + c * FPAST_POOL
    sel = jnp.where(pool == pt_ref[...], 1.0, 0.0).astype(BF16)
    hi, mid, lo = _split3(lf_ref[...])
    x_scr[...] += _dot(sel, hi) + (_dot(sel, mid) + _dot(sel, lo))

    @pl.when(c == pl.num_programs(1) - 1)
    def _():
        parts = _split3(x_scr[...])
        in_page = times(parts, within_ref[...])
        totals = _split3(times(parts, same_ref[...]))
        before = before_ref[...]
        f_ref[...] = in_page + (_dot(before, totals[0]) + (_dot(before, totals[1]) + _dot(before, totals[2])))


def _fpast(logf_view, page_table, l):
    db, n_pages = page_table.shape
    n_pool = logf_view.shape[1]
    rows = db * n_pages
    assert rows % FPAST_ROWS == 0 and FPAST_ROWS % n_pages == 0 and n_pool % FPAST_POOL == 0
    within, same, before = _fpast_mats(n_pages)
    const = lambda g, c: (0, 0)
    return pl.pallas_call(
        _fpast_kernel,
        grid=(rows // FPAST_ROWS, n_pool // FPAST_POOL),
        in_specs=[pl.BlockSpec((FPAST_ROWS, 1), lambda g, c: (g, 0)),
                  pl.BlockSpec((None, FPAST_POOL, FLAT), lambda g, c: (l, c, 0)),
                  pl.BlockSpec((FLAT, FLAT), const), pl.BlockSpec((FLAT, FLAT), const),
                  pl.BlockSpec((FPAST_ROWS, FPAST_ROWS), const)],
        out_specs=pl.BlockSpec((FPAST_ROWS, FLAT), lambda g, c: (g, 0)),
        out_shape=jax.ShapeDtypeStruct((rows, FLAT), F32),
        scratch_shapes=[pltpu.VMEM((FPAST_ROWS, FLAT), F32)],
        compiler_params=_cparams(("arbitrary", "arbitrary")),
        name="fpast",
    )(page_table.reshape(rows, 1), logf_view, within, same, before)


def _attn_sample_kernel(pt_ref, *refs):
    n = PAGES_PER_STEP
    k_refs, v_refs = refs[:n], refs[n:2 * n]
    (f_ref, tot_ref, qblk_ref, q_ref, kn_ref, vn_ref, lfn_ref, bmask_ref, nmask_ref, ncum_ref,
     o_ref, m_scr, l_scr, acc_scr) = refs[2 * n:]
    c = pl.program_id(1)
    nq = q_ref.shape[0]
    t_new = nq // ATT_HEADS

    @pl.when(c == 0)
    def _():
        m_scr[...] = jnp.full(m_scr.shape, NEG_BIG, F32)
        l_scr[...] = jnp.zeros_like(l_scr)
        acc_scr[...] = jnp.zeros_like(acc_scr)

    qblk = qblk_ref[...]
    scores = []
    for pg in range(n):
        kt = k_refs[pg][...].reshape(ATT_WIDTH, PAGE_SIZE).astype(BF16)
        f = f_ref[pg]
        scores.append(_dot(qblk, kt) - jnp.concatenate([f] * t_new, axis=0))
    m_blk = functools.reduce(jnp.maximum, scores)
    m_prev = m_scr[...]
    m_new = jnp.maximum(m_prev, jnp.max(m_blk, axis=1, keepdims=True))
    alpha = jnp.exp(m_prev - m_new)
    l_new = alpha * l_scr[...]
    acc = alpha * acc_scr[...]
    for pg in range(n):
        p = jnp.exp(scores[pg] - m_new)
        l_new = l_new + jnp.sum(p, axis=1, keepdims=True)
        vt = v_refs[pg][...].reshape(ATT_WIDTH, PAGE_SIZE).astype(BF16)
        acc = acc + _dot_nt(p.astype(BF16), vt)
    m_scr[...] = m_new
    l_scr[...] = l_new
    acc_scr[...] = acc

    @pl.when(c == pl.num_programs(1) - 1)
    def _():
        own = acc_scr[...] * bmask_ref[...]
        past = own[:, 0:HEAD_DIM]
        for h in range(1, ATT_HEADS):
            past = past + own[:, h * HEAD_DIM:(h + 1) * HEAD_DIM]
        hi, mid, lo = _split3(lfn_ref[...])
        ncum = ncum_ref[...]
        fnew = _dot(hi, ncum) + (_dot(mid, ncum) + _dot(lo, ncum)) + tot_ref[...]
        kn = kn_ref[...].astype(BF16)
        s = _dot_nt(q_ref[...], kn) - fnew + nmask_ref[...]
        m_prev = m_scr[...]
        m_new = jnp.maximum(m_prev, jnp.max(s, axis=1, keepdims=True))
        alpha = jnp.exp(m_prev - m_new)
        p = jnp.exp(s - m_new)
        l_fin = alpha * l_scr[...] + jnp.sum(p, axis=1, keepdims=True)
        o_ref[...] = (alpha * past + _dot(p.astype(BF16), vn_ref[...].astype(BF16))) / l_fin


def _attn_sample(q32, kn32, vn32, lfn, cache_kt, cache_vt, fpast, ftot, page_table, l):
    db, n_pages = page_table.shape
    n = PAGES_PER_STEP
    nq = q32.shape[1]
    r = np.arange(nq)
    col = np.arange(ATT_WIDTH)
    own = (col[None, :] // HEAD_DIM) == (r[:, None] % ATT_HEADS)
    qblk = jnp.where(jnp.asarray(own)[None], jnp.tile(q32, (1, 1, ATT_HEADS)), jnp.zeros((), q32.dtype))
    ok = ((r[None, :] % ATT_HEADS) == (r[:, None] % ATT_HEADS)) & ((r[None, :] // ATT_HEADS) <= (r[:, None] // ATT_HEADS))
    nmask = np.where(ok, 0.0, NEG_BIG).astype(np.float32)
    ncum = (((r[:, None] % ATT_HEADS) == (r[None, :] % ATT_HEADS)) & (r[:, None] <= r[None, :])).astype(np.float32)
    pmap = lambda i: (lambda b, c, pt: (l, pt[b * n_pages + c * n + i], 0, 0, 0))
    bmap = lambda b, c, pt: (b, 0, 0)
    const = lambda b, c, pt: (0, 0)
    page = lambda i: pl.BlockSpec((None, None, ATT_HEADS, HEAD_DIM, PAGE_SIZE), pmap(i))
    grid_spec = pltpu.PrefetchScalarGridSpec(
        num_scalar_prefetch=1,
        grid=(db, n_pages // n),
        in_specs=[page(i) for i in range(n)] + [page(i) for i in range(n)] + [
            pl.BlockSpec((None, n, 8, LANES), lambda b, c, pt: (b, c, 0, 0)),
            pl.BlockSpec((None, 1, nq), bmap),
            pl.BlockSpec((None, nq, ATT_WIDTH), bmap),
            pl.BlockSpec((None, nq, HEAD_DIM), bmap),
            pl.BlockSpec((None, nq, HEAD_DIM), bmap),
            pl.BlockSpec((None, nq, HEAD_DIM), bmap),
            pl.BlockSpec((None, 1, nq), bmap),
            pl.BlockSpec((nq, ATT_WIDTH), const),
            pl.BlockSpec((nq, nq), const),
            pl.BlockSpec((nq, nq), const),
        ],
        out_specs=pl.BlockSpec((None, nq, HEAD_DIM), bmap),
        scratch_shapes=[pltpu.VMEM((nq, 1), F32), pltpu.VMEM((nq, 1), F32), pltpu.VMEM((nq, ATT_WIDTH), F32)],
    )
    return pl.pallas_call(
        _attn_sample_kernel,
        grid_spec=grid_spec,
        out_shape=jax.ShapeDtypeStruct((db, nq, HEAD_DIM), F32),
        compiler_params=_cparams(("arbitrary", "arbitrary")),
        name="attn_sample",
    )(page_table.reshape(-1), *([cache_kt] * n), *([cache_vt] * n), fpast, ftot, qblk, q32, kn32, vn32, lfn,
      jnp.asarray(own.astype(np.float32)), jnp.asarray(nmask), jnp.asarray(ncum, dtype=BF16))


def _merge_kernel(ys_ref, ya_ref, x_ref, gate1_ref, shift2_ref, scale2_ref, gs_ref, ga_ref, g2_ref,
                  wglu_ref, bglu_ref, wos_ref, woa_ref, wr_ref, br_ref, cnt_in_ref, strict_ref,
                  x1_ref, h2_ref, eidx_ref, gate_ref, rank_ref, cnt_out_ref, cnt_scr, *, hi_prec):
    first = (pl.program_id(0) == 0) & (pl.program_id(1) == 0)

    @pl.when(first)
    def _():
        cnt_scr[...] = cnt_in_ref[...]

    mm = _dot_x3 if hi_prec else (lambda a, b: _dot(a.astype(BF16), b))
    y = jax.nn.gelu(ys_ref[...])
    y = y * jax.nn.sigmoid(mm(y, wglu_ref[...]) + bglu_ref[...])
    mix = mm(_rms(y, gs_ref[...]), wos_ref[...]) + mm(_rms(ya_ref[...], ga_ref[...]), woa_ref[...])
    x1 = x_ref[...] + gate1_ref[...] * mix
    x1_ref[...] = x1
    h2 = _rms(x1, g2_ref[...]) * (1.0 + scale2_ref[...]) + shift2_ref[...]
    h2_ref[...] = h2
    logits = _dot_x3(h2, wr_ref[...]) + br_ref[...]
    rows = logits.shape[0]
    lane = lax.broadcasted_iota(jnp.int32, (rows, LANES), 1)
    lane_f = lane.astype(F32)
    vals, idxs = [], []
    for _ in range(TOP_K):
        m = jnp.max(logits, axis=1, keepdims=True)
        idx = jnp.min(jnp.where(logits == m, lane_f, float(LANES)), axis=1, keepdims=True)
        vals.append(m)
        idxs.append(idx)
        logits = jnp.where(lane_f == idx, -jnp.inf, logits)
    exps = [jnp.exp(v - vals[0]) for v in vals]
    den = exps[0] + exps[1] + exps[2] + exps[3]
    cnt = cnt_scr[...]
    strict = strict_ref[...]
    e_out = jnp.zeros((rows, LANES), F32)
    g_out = jnp.zeros((rows, LANES), F32)
    r_out = jnp.zeros((rows, LANES), F32)
    for kk in range(TOP_K):
        onehot = lane_f == idxs[kk]
        oh = jnp.where(onehot, 1.0, 0.0)
        before = _dot(strict, oh.astype(BF16)) + cnt
        rank = jnp.sum(jnp.where(onehot, before, 0.0), axis=1, keepdims=True)
        cnt = cnt + jnp.sum(oh, axis=0, keepdims=True)
        e_out = jnp.where(lane == kk, idxs[kk], e_out)
        g_out = jnp.where(lane == kk, exps[kk] / den, g_out)
        r_out = jnp.where(lane == kk, rank, r_out)
    cnt_scr[...] = cnt
    cnt_out_ref[...] = cnt
    eidx_ref[...] = e_out.astype(jnp.int32)
    gate_ref[...] = g_out
    rank_ref[...] = r_out.astype(jnp.int32)


def _merge(ys, ya, x, gate1, shift2, scale2, g_out_ssm, g_out_att, g2, w_glu, b_glu, w_out_s, w_out_a,
           w_router, b_router, cnt_in, *, tile, hi_prec):
    s_dim, r_dim, _ = x.shape
    rm = gate1.shape[1]
    row = lambda s, t: (s, t, 0)
    const2 = lambda s, t: (0, 0)
    mod_map = (lambda s, t: (s, 0, 0)) if rm == 1 else row
    mod_blk = (None, 1, D_MODEL) if rm == 1 else (None, tile, D_MODEL)
    full = lambda a: pl.BlockSpec(a.shape, const2)
    half = pl.BlockSpec((None, tile, SSM_WIDTH), row)
    wide = pl.BlockSpec((None, tile, D_MODEL), row)
    narrow = pl.BlockSpec((None, tile, LANES), row)
    strict = _tri_matrix(tile, strict=True)
    args = (ys, ya, x, gate1, shift2, scale2, g_out_ssm, g_out_att, g2, w_glu, b_glu, w_out_s, w_out_a,
            w_router, b_router, cnt_in, strict)
    return pl.pallas_call(
        functools.partial(_merge_kernel, hi_prec=hi_prec),
        grid=(s_dim, r_dim // tile),
        in_specs=[half, half, wide, pl.BlockSpec(mod_blk, mod_map), pl.BlockSpec(mod_blk, mod_map),
                  pl.BlockSpec(mod_blk, mod_map)] + [full(a) for a in args[6:]],
        out_specs=[wide, wide, narrow, narrow, narrow, pl.BlockSpec((1, LANES), const2)],
        out_shape=[jax.ShapeDtypeStruct((s_dim, r_dim, D_MODEL), F32),
                   jax.ShapeDtypeStruct((s_dim, r_dim, D_MODEL), F32),
                   jax.ShapeDtypeStruct((s_dim, r_dim, LANES), jnp.int32),
                   jax.ShapeDtypeStruct((s_dim, r_dim, LANES), F32),
                   jax.ShapeDtypeStruct((s_dim, r_dim, LANES), jnp.int32),
                   jax.ShapeDtypeStruct((1, LANES), F32)],
        scratch_shapes=[pltpu.VMEM((1, LANES), F32)],
        compiler_params=_cparams(("arbitrary", "arbitrary")),
        name="merge_hi" if hi_prec else "merge",
    )(*args)


MOE_TILE = 256
ROW_TILE = 128
DMA_UNROLL = 8


def _dispatch_kernel(fill_ref, dest_ref, hp_ref, hs_ref, xpad_ref, zero_scr, sem):
    i = pl.program_id(0)
    last = pl.num_programs(0) - 1
    rows = hp_ref.shape[0]

    def scatter(h_ref):
        def row_copy(src_row, dst_row):
            return pltpu.make_async_copy(h_ref.at[pl.ds(src_row, 1)], xpad_ref.at[pl.ds(dst_row, 1)], sem)

        def issue(a, carry):
            row_copy(a // TOP_K, dest_ref[0, a]).start()
            return carry

        lax.fori_loop(0, rows * TOP_K, issue, 0, unroll=DMA_UNROLL)

        def drain(a, carry):
            row_copy(0, 0).wait()
            return carry

        lax.fori_loop(0, rows * TOP_K, drain, 0, unroll=DMA_UNROLL)

    @pl.when(i < last)
    def _():
        scatter(hp_ref)

    @pl.when(i == last)
    def _():
        scatter(hs_ref)
        zero_scr[...] = jnp.zeros_like(zero_scr)

        def zero_row(dst_row):
            return pltpu.make_async_copy(zero_scr.at[pl.ds(0, 1)], xpad_ref.at[pl.ds(dst_row, 1)], sem)

        def per_expert(e, carry):
            start, count = fill_ref[2 * e], fill_ref[2 * e + 1]

            def fill(a, c2):
                zero_row(start + a).start()
                return c2

            lax.fori_loop(0, count, fill, 0)

            def fill_wait(a, c2):
                zero_row(0).wait()
                return c2

            lax.fori_loop(0, count, fill_wait, 0)
            return carry

        lax.fori_loop(0, N_EXPERTS, per_expert, 0)

        def zero_tile(t):
            return pltpu.make_async_copy(zero_scr, xpad_ref.at[pl.ds(t * MOE_TILE, MOE_TILE)], sem)

        n_used = fill_ref[2 * N_EXPERTS]
        n_tiles = xpad_ref.shape[0] // MOE_TILE

        def tail(t, carry):
            zero_tile(t).start()
            return carry

        lax.fori_loop(n_used, n_tiles, tail, 0)

        def tail_wait(t, carry):
            zero_tile(0).wait()
            return carry

        lax.fori_loop(n_used, n_tiles, tail_wait, 0)


def _dispatch(h2p, h2s, dest, fill, cap):
    ntp = h2p.shape[0] // ROW_TILE
    nt = ntp + 1
    grid_spec = pltpu.PrefetchScalarGridSpec(
        num_scalar_prefetch=1,
        grid=(nt,),
        in_specs=[pl.BlockSpec((None, 1, ROW_TILE * TOP_K), lambda i, f: (i, 0, 0), memory_space=pltpu.SMEM),
                  pl.BlockSpec((ROW_TILE, D_MODEL), lambda i, f: (jnp.minimum(i, ntp - 1), 0)),
                  pl.BlockSpec((ROW_TILE, D_MODEL), lambda i, f: (0, 0))],
        out_specs=pl.BlockSpec(memory_space=pl.ANY),
        scratch_shapes=[pltpu.VMEM((MOE_TILE, D_MODEL), F32), pltpu.SemaphoreType.DMA(())],
    )
    return pl.pallas_call(
        _dispatch_kernel,
        grid_spec=grid_spec,
        out_shape=jax.ShapeDtypeStruct((cap, D_MODEL), F32),
        compiler_params=_cparams(("arbitrary",)),
        name="dispatch",
    )(fill, dest.reshape(nt, 1, ROW_TILE * TOP_K), h2p, h2s)


def _experts_kernel(te_ref, nu_ref, x_ref, wgu_ref, bgu_ref, wd_ref, bd_ref, y_ref, wgu_scr, wd_scr):
    i = pl.program_id(0)
    prev = te_ref[jnp.maximum(i - 1, 0)]
    fresh = (i == 0) | (te_ref[i] != prev)

    @pl.when(fresh)
    def _():
        wgu_scr[...] = wgu_ref[...].astype(BF16)
        wd_scr[...] = wd_ref[...].astype(BF16)

    @pl.when(i < nu_ref[0])
    def _():
        gu = _dot(x_ref[...].astype(BF16), wgu_scr[...]) + bgu_ref[...]
        g = jnp.minimum(gu[:, :D_EXPERT], SWIGLU_LIMIT)
        up = jnp.clip(gu[:, D_EXPERT:], -SWIGLU_LIMIT, SWIGLU_LIMIT)
        act = (up + 1.0) * (g * jax.nn.sigmoid(SWIGLU_ALPHA * g))
        y_ref[...] = _dot(act.astype(BF16), wd_scr[...]) + bd_ref[...]

    @pl.when(i >= nu_ref[0])
    def _():
        y_ref[...] = jnp.zeros_like(y_ref)


def _experts(xpad, tile_expert, n_used, w_gate_up, b_gate_up, w_down, b_down, l):
    cap = xpad.shape[0]
    n_tiles = cap // MOE_TILE
    tmap = lambda i, te, nu: (jnp.minimum(i, nu[0] - 1), 0)
    grid_spec = pltpu.PrefetchScalarGridSpec(
        num_scalar_prefetch=2,
        grid=(n_tiles,),
        in_specs=[
            pl.BlockSpec((MOE_TILE, D_MODEL), tmap),
            pl.BlockSpec((None, None, D_MODEL, 2 * D_EXPERT), lambda i, te, nu: (l, te[i], 0, 0)),
            pl.BlockSpec((None, None, 1, 2 * D_EXPERT), lambda i, te, nu: (l, te[i], 0, 0)),
            pl.BlockSpec((None, None, D_EXPERT, D_MODEL), lambda i, te, nu: (l, te[i], 0, 0)),
            pl.BlockSpec((None, None, 1, D_MODEL), lambda i, te, nu: (l, te[i], 0, 0)),
        ],
        out_specs=pl.BlockSpec((MOE_TILE, D_MODEL), lambda i, te, nu: (i, 0)),
        scratch_shapes=[pltpu.VMEM((D_MODEL, 2 * D_EXPERT), BF16), pltpu.VMEM((D_EXPERT, D_MODEL), BF16)],
    )
    return pl.pallas_call(
        _experts_kernel,
        grid_spec=grid_spec,
        out_shape=jax.ShapeDtypeStruct((cap, D_MODEL), F32),
        compiler_params=_cparams(("arbitrary",)),
        name="experts",
    )(tile_expert, n_used, xpad, w_gate_up, b_gate_up.reshape(DEPTH, N_EXPERTS, 1, 2 * D_EXPERT),
      w_down, b_down.reshape(DEPTH, N_EXPERTS, 1, D_MODEL))


def _combine_kernel(dest_ref, x1_ref, gate2_ref, g_ref, ypad_ref, o_ref, buf, sem):
    rows = x1_ref.shape[0]

    def row_copy(src_row, slot):
        return pltpu.make_async_copy(ypad_ref.at[pl.ds(src_row, 1)], buf.at[pl.ds(slot, 1)], sem)

    def issue(a, carry):
        row_copy(dest_ref[0, a], (a % TOP_K) * rows + a // TOP_K).start()
        return carry

    lax.fori_loop(0, rows * TOP_K, issue, 0, unroll=DMA_UNROLL)

    def drain(a, carry):
        row_copy(0, 0).wait()
        return carry

    lax.fori_loop(0, rows * TOP_K, drain, 0, unroll=DMA_UNROLL)
    g = g_ref[...]
    moe = jnp.zeros((rows, D_MODEL), F32)
    for kk in range(TOP_K):
        moe = moe + g[:, kk:kk + 1] * buf[kk * rows:(kk + 1) * rows, :]
    o_ref[...] = x1_ref[...] + gate2_ref[...] * moe


def _combine(x1, gate2, gates, dest, ypad):
    s_dim, r_dim, _ = x1.shape
    rm = gate2.shape[1]
    nt = r_dim // ROW_TILE
    row = lambda s, t: (s, t, 0)
    mod_map = (lambda s, t: (s, 0, 0)) if rm == 1 else row
    mod_blk = (None, 1, D_MODEL) if rm == 1 else (None, ROW_TILE, D_MODEL)
    return pl.pallas_call(
        _combine_kernel,
        grid=(s_dim, nt),
        in_specs=[pl.BlockSpec((None, 1, ROW_TILE * TOP_K), lambda s, t: (s * nt + t, 0, 0),
                               memory_space=pltpu.SMEM),
                  pl.BlockSpec((None, ROW_TILE, D_MODEL), row), pl.BlockSpec(mod_blk, mod_map),
                  pl.BlockSpec((None, ROW_TILE, LANES), row), pl.BlockSpec(memory_space=pl.ANY)],
        out_specs=pl.BlockSpec((None, ROW_TILE, D_MODEL), row),
        out_shape=jax.ShapeDtypeStruct(x1.shape, F32),
        scratch_shapes=[pltpu.VMEM((TOP_K * ROW_TILE, D_MODEL), F32), pltpu.SemaphoreType.DMA(())],
        compiler_params=_cparams(("arbitrary", "arbitrary")),
        name="combine",
    )(dest.reshape(s_dim * nt, 1, ROW_TILE * TOP_K), x1, gate2, gates, ypad)


def _mods(mod_rows):
    return [m for m in jnp.split(mod_rows, N_MOD, axis=-1)]


def kernel(x_prompt, x_sample, c_prompt, c_sample, cache_k, cache_v, cache_logf, state_ssm_re, state_ssm_im,
           page_table, w_ada, b_ada, g_norm1, g_norm2, w_in, b_forget, g_q, g_k, ssm_lam_re, ssm_lam_im,
           ssm_log_dt, ssm_b_re, ssm_b_im, ssm_c_re, ssm_c_im, ssm_d, w_glu, b_glu, g_out_ssm, g_out_att, w_out,
           w_router, b_router, w_gate_up, b_gate_up, w_down, b_down):
    bsz, seq, _ = x_prompt.shape
    db, t_new, _ = x_sample.shape
    n_pool = cache_k.shape[1]
    n_seq = bsz + db
    c_rows = -(-n_seq // 8) * 8
    c_all = jnp.concatenate([c_prompt, c_sample, jnp.zeros((c_rows - n_seq, D_MODEL), F32)], axis=0)
    mod = _adaln(c_all, w_ada, b_ada)
    prep = _ssm_prep(ssm_lam_re, ssm_lam_im, ssm_log_dt, ssm_b_re, ssm_b_im, ssm_c_re, ssm_c_im, ssm_d)
    logf_view = cache_logf.reshape(DEPTH, n_pool, FLAT)
    cache_kt = jnp.transpose(cache_k, (0, 1, 3, 4, 2))
    cache_vt = jnp.transpose(cache_v, (0, 1, 3, 4, 2))
    n_pages = page_table.shape[1]
    st_re = state_ssm_re.reshape(DEPTH, db, SSM_GROUPS * SSM_STATE)
    st_im = state_ssm_im.reshape(DEPTH, db, SSM_GROUPS * SSM_STATE)
    n_tok_p, n_tok_s = bsz * seq, db * t_new
    assert n_tok_s == ROW_TILE and n_tok_p % ROW_TILE == 0
    n_tiles = -(-(n_tok_p + n_tok_s) * TOP_K // MOE_TILE) + N_EXPERTS
    cap = n_tiles * MOE_TILE
    pad_f = LANES - ATT_HEADS
    pad_e = LANES - N_EXPERTS

    xp = x_prompt
    xs = jnp.swapaxes(x_sample, 0, 1)
    outs = {name: [] for name in ('kp', 'vp', 'lfp', 'srp', 'sip', 'ks', 'vs', 'lfs', 'srs', 'sis')}
    for l in range(DEPTH):
        mp = [m[:, None, :] for m in _mods(mod[l, :bsz])]
        ms = [jnp.broadcast_to(m[None], (t_new, db, D_MODEL)) for m in _mods(mod[l, bsz:n_seq])]
        w_main = w_in[l, :, :IN_MAIN]
        w_f = jnp.pad(w_in[l, :, IN_MAIN:], ((0, 0), (0, pad_f)))
        b_f = jnp.pad(b_forget[l], (0, pad_f))[None]
        gq = jnp.tile(g_q[l], ATT_HEADS)[None]
        gk = jnp.tile(g_k[l], ATT_HEADS)[None]
        g1 = g_norm1[l][None]
        g2 = g_norm2[l][None]
        w_os, w_oa = w_out[l, :SSM_WIDTH], w_out[l, SSM_WIDTH:]
        w_r = jnp.pad(w_router[l], ((0, 0), (0, pad_e)))
        b_r = jnp.pad(b_router[l], (0, pad_e), constant_values=NEG_BIG)[None]
        small = (g_out_ssm[l][None], g_out_att[l][None], g2)

        u, q, k, v, kcat, vcat, lf, fc = _in_proj(xp, mp[0], mp[1], g1, w_main.astype(BF16), w_f.astype(BF16),
                                                  b_f, gq, gk, tile=512, hi_prec=False, with_cumsum=True)
        y_ssm, s_re, s_im = _ssm_prompt(u, prep, l)
        bound = float(HEAD_DIM) * ATT_SCALE * jnp.max(jnp.abs(g_q[l])) * jnp.max(jnp.abs(g_k[l]))
        y_att = lax.cond(
            bound <= SHIFT_LIMIT,
            lambda: _attn_bounded(q, kcat, vcat, fc - bound),
            lambda: _attn_prompt(q, kcat, vcat, fc, jnp.swapaxes(fc[..., :ATT_HEADS], 1, 2)))
        x1p, h2p, e_p, gate_p, rank_p, cnt_p = _merge(
            y_ssm, y_att, xp, mp[2], mp[3], mp[4], *small, w_glu[l].astype(BF16), b_glu[l][None],
            w_os.astype(BF16), w_oa.astype(BF16), w_r, b_r, jnp.zeros((1, LANES), F32), tile=512, hi_prec=False)
        outs['kp'].append(k.reshape(bsz, seq, ATT_HEADS, HEAD_DIM))
        outs['vp'].append(v.reshape(bsz, seq, ATT_HEADS, HEAD_DIM))
        outs['lfp'].append(lf[..., :ATT_HEADS])
        outs['srp'].append(s_re)
        outs['sip'].append(s_im)

        u_s, q_s, k_s, v_s, _, _, lf_s, _ = _in_proj(xs, ms[0], ms[1], g1, w_main, w_f, b_f, gq, gk,
                                                     tile=db, hi_prec=True, with_cumsum=False)
        y_ssm_s, o_re, o_im = _ssm_sample(u_s, st_re, st_im, prep, l)
        per_seq = lambda a: jnp.swapaxes(a, 0, 1).reshape(db, t_new * ATT_HEADS, HEAD_DIM)
        lfn = jnp.swapaxes(lf_s[..., :ATT_HEADS], 0, 1).reshape(db, 1, t_new * ATT_HEADS)
        fpast = _fpast(logf_view, page_table, l).reshape(db, n_pages, ATT_HEADS, PAGE_SIZE)
        ftot = jnp.tile(fpast[:, -1, :, -1], (1, t_new)).reshape(db, 1, t_new * ATT_HEADS)
        o_s = _attn_sample(per_seq(q_s), per_seq(k_s), per_seq(v_s), lfn, cache_kt, cache_vt, fpast, ftot,
                           page_table, l)
        y_att_s = jnp.swapaxes(o_s.reshape(db, t_new, ATT_WIDTH), 0, 1)
        x1s, h2s, e_s, gate_s, rank_s, cnt = _merge(
            y_ssm_s, y_att_s, xs, ms[2], ms[3], ms[4], *small, w_glu[l], b_glu[l][None], w_os, w_oa, w_r, b_r,
            cnt_p, tile=db, hi_prec=True)
        outs['ks'].append(jnp.swapaxes(k_s, 0, 1).reshape(db, t_new, ATT_HEADS, HEAD_DIM))
        outs['vs'].append(jnp.swapaxes(v_s, 0, 1).reshape(db, t_new, ATT_HEADS, HEAD_DIM))
        outs['lfs'].append(jnp.swapaxes(lf_s[..., :ATT_HEADS], 0, 1))
        outs['srs'].append(o_re.reshape(db, SSM_GROUPS, SSM_STATE))
        outs['sis'].append(o_im.reshape(db, SSM_GROUPS, SSM_STATE))

        counts = cnt[0, :N_EXPERTS].astype(jnp.int32)
        padded = (counts + MOE_TILE - 1) // MOE_TILE * MOE_TILE
        pend = jnp.cumsum(padded)
        pstart = pend - padded
        experts = jnp.arange(N_EXPERTS, dtype=jnp.int32)

        def slots(e, rank):
            hit = e[..., :TOP_K, None] == experts
            return (jnp.sum(jnp.where(hit, pstart, 0), axis=-1) + rank[..., :TOP_K]).reshape(-1)

        dest_p = slots(e_p, rank_p)
        dest_s = slots(e_s, rank_s)
        tile_start = jnp.arange(n_tiles, dtype=jnp.int32) * MOE_TILE
        tile_expert = jnp.sum((pend[None, :] <= tile_start[:, None]).astype(jnp.int32), axis=1)
        tile_expert = jnp.minimum(tile_expert, N_EXPERTS - 1)
        n_used = (pend[-1:] // MOE_TILE).astype(jnp.int32)
        fill = jnp.stack([pstart + counts, padded - counts], axis=-1).reshape(-1)
        fill = jnp.concatenate([fill, n_used]).astype(jnp.int32)
        xpad = _dispatch(h2p.reshape(n_tok_p, D_MODEL), h2s.reshape(n_tok_s, D_MODEL),
                         jnp.concatenate([dest_p, dest_s]), fill, cap)
        ypad = _experts(xpad, tile_expert, n_used, w_gate_up, b_gate_up, w_down, b_down, l)
        xp = _combine(x1p, mp[5], gate_p, dest_p, ypad)
        xs = _combine(x1s.reshape(1, n_tok_s, D_MODEL), ms[5].reshape(1, n_tok_s, D_MODEL),
                      gate_s.reshape(1, n_tok_s, LANES), dest_s, ypad).reshape(t_new, db, D_MODEL)

    st = lambda name: jnp.stack(outs[name])
    return (xp, jnp.swapaxes(xs, 0, 1), st('kp'), st('vp'), st('lfp'), st('srp'), st('sip'),
            st('ks'), st('vs'), st('lfs'), st('srs'), st('sis'))
```

```python
import functools
import math

import jax
import jax.numpy as jnp
import numpy as np
from jax import lax
from jax.experimental import pallas as pl
from jax.experimental.pallas import tpu as pltpu

F32 = jnp.float32
BF16 = jnp.bfloat16

D_MODEL = 1024
DEPTH = 4
PAGE_SIZE = 128
SSM_WIDTH = 512
SSM_GROUP_CH = 16
SSM_GROUPS = 32
SSM_STATE = 64
ATT_WIDTH = 512
HEAD_DIM = 64
ATT_HEADS = 8
ATT_SCALE = HEAD_DIM ** -0.5
IN_MAIN = SSM_WIDTH + 3 * ATT_WIDTH
N_EXPERTS = 32
TOP_K = 4
D_EXPERT = 1024
SWIGLU_LIMIT = 7.0
SWIGLU_ALPHA = 1.702
N_MOD = 6
NORM_EPS = 1e-6
LOG2E = math.log2(math.e)

LANES = 128
SSM_CHUNK = 8
SSM_BLOCKS = SSM_WIDTH // LANES
GROUPS_PER_BLOCK = LANES // SSM_GROUP_CH
STATES_PER_BLOCK = GROUPS_PER_BLOCK * SSM_STATE
VMEM_LIMIT = 56 * 1024 * 1024


def _cparams(sem):
    return pltpu.CompilerParams(dimension_semantics=sem, vmem_limit_bytes=VMEM_LIMIT)


def _split2(x):
    hi = x.astype(BF16)
    lo = (x - hi.astype(F32)).astype(BF16)
    return hi, lo


def _split3(x):
    hi = x.astype(BF16)
    r = x - hi.astype(F32)
    mid = r.astype(BF16)
    lo = (r - mid.astype(F32)).astype(BF16)
    return hi, mid, lo


def _dot(a, b):
    return jnp.dot(a, b, preferred_element_type=F32)


def _dot_x3(a, b):
    ah, al = _split2(a)
    bh, bl = _split2(b)
    return _dot(ah, bh) + (_dot(ah, bl) + _dot(al, bh))


def _dot_nt(a, b):
    return lax.dot_general(a, b, (((1,), (1,)), ((), ())), preferred_element_type=F32)


def _rms(x, g):
    return x * lax.rsqrt(jnp.mean(x * x, axis=-1, keepdims=True) + NORM_EPS) * g


def _adaln_kernel(c_ref, w_ref, b_ref, o_ref):
    c = c_ref[...]
    a = c * jax.nn.sigmoid(c)
    o_ref[...] = _dot_x3(a, w_ref[...]) + b_ref[...]


def _adaln(c_all, w_ada, b_ada):
    rows = c_all.shape[0]
    tn = 1536
    nt = (N_MOD * D_MODEL) // tn
    return pl.pallas_call(
        _adaln_kernel,
        grid=(DEPTH, nt),
        in_specs=[
            pl.BlockSpec((rows, D_MODEL), lambda l, n: (0, 0)),
            pl.BlockSpec((None, D_MODEL, tn), lambda l, n: (l, 0, n)),
            pl.BlockSpec((None, 1, tn), lambda l, n: (l, 0, n)),
        ],
        out_specs=pl.BlockSpec((None, rows, tn), lambda l, n: (l, 0, n)),
        out_shape=jax.ShapeDtypeStruct((DEPTH, rows, N_MOD * D_MODEL), F32),
        compiler_params=_cparams(("arbitrary", "arbitrary")),
        name="adaln",
    )(c_all, w_ada, b_ada.reshape(DEPTH, 1, N_MOD * D_MODEL))


def _log_sigmoid(x):
    return jnp.minimum(x, 0.0) - jnp.log1p(jnp.exp(-jnp.abs(x)))


def _inproj_kernel(x_ref, shift_ref, scale_ref, g1_ref, wm_ref, wf_ref, bf_ref, gq_ref, gk_ref,
                   seg_ref, tri_ref, place_ref, kone_ref,
                   u_ref, q_ref, k_ref, v_ref, kcat_ref, vcat_ref, lf_ref, fc_ref, carry_ref,
                   *, hi_prec, with_cumsum):
    x = x_ref[...]
    h = _rms(x, g1_ref[...]) * (1.0 + scale_ref[...]) + shift_ref[...]
    if hi_prec:
        z = _dot_x3(h, wm_ref[...])
        zf = _dot_x3(h, wf_ref[...])
    else:
        hb = h.astype(BF16)
        z = _dot(hb, wm_ref[...])
        zf = _dot(hb, wf_ref[...])
    u_ref[...] = z[:, :SSM_WIDTH]
    q = z[:, SSM_WIDTH:SSM_WIDTH + ATT_WIDTH]
    k = z[:, SSM_WIDTH + ATT_WIDTH:SSM_WIDTH + 2 * ATT_WIDTH]
    v = z[:, SSM_WIDTH + 2 * ATT_WIDTH:]
    seg = seg_ref[...]

    def head_norm(t, g):
        hi, lo = _split2(t * t)
        ms = _dot(hi, seg) + _dot(lo, seg)
        return t * lax.rsqrt(ms + NORM_EPS) * g

    qn = head_norm(q, gq_ref[...])
    kn = head_norm(k, gk_ref[...])
    q_ref[...] = (qn * (ATT_SCALE * LOG2E if with_cumsum else ATT_SCALE)).astype(BF16)
    k_ref[...] = kn
    v_ref[...] = v
    logf = _log_sigmoid(zf + bf_ref[...])
    lf_ref[...] = logf
    if with_cumsum:
        @pl.when(pl.program_id(1) == 0)
        def _():
            carry_ref[...] = jnp.zeros_like(carry_ref)

        tri = tri_ref[...]
        hi, mid, lo = _split3(logf)
        fc = carry_ref[...] + (_dot(tri, hi) + (_dot(tri, mid) + _dot(tri, lo)))
        carry_ref[...] = fc[-1:, :]
        fc = fc * LOG2E
        fc_ref[...] = fc
        hi, mid, lo = _split3(fc)
        kext = (_dot(hi, place_ref[0]) + (_dot(mid, place_ref[1]) + _dot(lo, place_ref[2]))
                + kone_ref[...]).astype(BF16)
        kb = kn.astype(BF16)
        vb = v.astype(BF16)
        lane = lax.broadcasted_iota(jnp.int32, (x.shape[0], LANES), 1)
        vext = jnp.where(lane == 0, 1.0, 0.0).astype(BF16)
        for p in range(ATT_HEADS // 2):
            lo_l, hi_l = p * LANES, (p + 1) * LANES
            kcat_ref[:, 2 * lo_l:2 * lo_l + LANES] = kb[:, lo_l:hi_l]
            kcat_ref[:, 2 * lo_l + LANES:2 * hi_l] = kext[:, lo_l:hi_l]
            vcat_ref[:, 2 * lo_l:2 * lo_l + LANES] = vb[:, lo_l:hi_l]
            vcat_ref[:, 2 * lo_l + LANES:2 * hi_l] = vext
    else:
        fc_ref[...] = logf
        kcat_ref[...] = jnp.zeros_like(kcat_ref)
        vcat_ref[...] = jnp.zeros_like(vcat_ref)


def _seg_matrix():
    i = np.arange(ATT_WIDTH) // HEAD_DIM
    return jnp.asarray((i[:, None] == i[None, :]).astype(np.float32) / HEAD_DIM, dtype=BF16)


def _tri_matrix(n, strict=False):
    i = np.arange(n)
    m = (i[None, :] < i[:, None]) if strict else (i[None, :] <= i[:, None])
    return jnp.asarray(m.astype(np.float32), dtype=BF16)


N_BIAS = 3


def _bias_placement():
    place = np.zeros((N_BIAS, LANES, ATT_WIDTH), np.float32)
    kone = np.zeros((1, ATT_WIDTH), np.float32)
    for h in range(ATT_HEADS):
        base = (h // 2) * LANES
        for part in range(N_BIAS):
            place[part, h, base + N_BIAS * (1 + h % 2) + part] = -1.0
    for p in range(ATT_HEADS // 2):
        kone[0, p * LANES:p * LANES + N_BIAS] = 1.0
    return jnp.asarray(place, dtype=BF16), jnp.asarray(kone)


def _in_proj(x, shift, scale, g1, w_main, w_f, b_f, g_q, g_k, *, tile, hi_prec, with_cumsum):
    s_dim, r_dim, _ = x.shape
    rm = shift.shape[1]
    nt = r_dim // tile
    row = lambda s, t: (s, t, 0)
    const2 = lambda s, t: (0, 0)
    mod_map = (lambda s, t: (s, 0, 0)) if rm == 1 else row
    mod_blk = (None, 1, D_MODEL) if rm == 1 else (None, tile, D_MODEL)
    outs = [
        jax.ShapeDtypeStruct((s_dim, r_dim, SSM_WIDTH), F32),
        jax.ShapeDtypeStruct((s_dim, r_dim, ATT_WIDTH), BF16),
        jax.ShapeDtypeStruct((s_dim, r_dim, ATT_WIDTH), F32),
        jax.ShapeDtypeStruct((s_dim, r_dim, ATT_WIDTH), F32),
        jax.ShapeDtypeStruct((s_dim, r_dim, 2 * ATT_WIDTH), BF16),
        jax.ShapeDtypeStruct((s_dim, r_dim, 2 * ATT_WIDTH), BF16),
        jax.ShapeDtypeStruct((s_dim, r_dim, LANES), F32),
        jax.ShapeDtypeStruct((s_dim, r_dim, LANES), F32),
    ]
    wide = pl.BlockSpec((None, tile, ATT_WIDTH), row)
    cat = pl.BlockSpec((None, tile, 2 * ATT_WIDTH), row)
    narrow = pl.BlockSpec((None, tile, LANES), row)
    place, kone = _bias_placement()
    return pl.pallas_call(
        functools.partial(_inproj_kernel, hi_prec=hi_prec, with_cumsum=with_cumsum),
        grid=(s_dim, nt),
        in_specs=[
            pl.BlockSpec((None, tile, D_MODEL), row),
            pl.BlockSpec(mod_blk, mod_map),
            pl.BlockSpec(mod_blk, mod_map),
            pl.BlockSpec((1, D_MODEL), const2),
            pl.BlockSpec(w_main.shape, const2),
            pl.BlockSpec(w_f.shape, const2),
            pl.BlockSpec((1, LANES), const2),
            pl.BlockSpec((1, ATT_WIDTH), const2),
            pl.BlockSpec((1, ATT_WIDTH), const2),
            pl.BlockSpec((ATT_WIDTH, ATT_WIDTH), const2),
            pl.BlockSpec((tile, tile), const2),
            pl.BlockSpec(place.shape, lambda s, t: (0, 0, 0)),
            pl.BlockSpec(kone.shape, const2),
        ],
        out_specs=[wide, wide, wide, wide, cat, cat, narrow, narrow],
        out_shape=outs,
        scratch_shapes=[pltpu.VMEM((1, LANES), F32)],
        compiler_params=_cparams(("arbitrary", "arbitrary")),
        name="in_proj_hi" if hi_prec else "in_proj",
    )(x, shift, scale, g1, w_main, w_f, b_f, g_q, g_k, _seg_matrix(), _tri_matrix(tile), place, kone)


def _discretize(lam_re, lam_im, log_dt):
    dt = jnp.exp(log_dt)
    mag = jnp.exp(lam_re * dt)
    ang = lam_im * dt
    a_re = mag * jnp.cos(ang)
    a_im = mag * jnp.sin(ang)
    den = lam_re * lam_re + lam_im * lam_im
    nr = a_re - 1.0
    coef_re = (nr * lam_re + a_im * lam_im) / den
    coef_im = (a_im * lam_re - nr * lam_im) / den
    return a_re, a_im, coef_re, coef_im


def _powers(a_re, a_im, n):
    pr, pi = [jnp.ones_like(a_re)], [jnp.zeros_like(a_im)]
    for _ in range(n):
        r, i = pr[-1], pi[-1]
        pr.append(r * a_re - i * a_im)
        pi.append(r * a_im + i * a_re)
    return pr, pi


def _ssm_prep_kernel(lr_row, li_row, dt_row, lr_col, li_col, dt_col, btr_ref, bti_ref, ctr_ref, cti_ref,
                     d_ref, tmat_ref, wz_ref, wc_ref, apow_ref, bbt_ref):
    L, W = SSM_CHUNK, LANES
    ar, ai, cr, ci = _discretize(lr_row[...], li_row[...], dt_row[...])
    acr, aci, _, _ = _discretize(lr_col[...], li_col[...], dt_col[...])
    btr, bti = btr_ref[...], bti_ref[...]
    bb_re = cr * btr - ci * bti
    bb_im = cr * bti + ci * btr
    bbt_ref[0] = bb_re
    bbt_ref[1] = bb_im
    ctr, cti = ctr_ref[...], cti_ref[...]
    pr, pi = _powers(ar, ai, L)
    pcr, pci = _powers(acr, aci, L)
    apow_ref[...] = jnp.concatenate(
        [ar, ai, pr[L], pi[L], jnp.zeros((4, STATES_PER_BLOCK), F32)], axis=0)
    eye = (lax.broadcasted_iota(jnp.int32, (W, W), 0) == lax.broadcasted_iota(jnp.int32, (W, W), 1))
    kts = []
    for tau in range(L):
        m_re = pr[tau] * bb_re - pi[tau] * bb_im
        m_im = pr[tau] * bb_im + pi[tau] * bb_re
        kt = _dot_x3(m_re, ctr) - _dot_x3(m_im, cti)
        if tau == 0:
            kt = kt + jnp.where(eye, d_ref[...], 0.0)
        kts.append(kt.astype(BF16))
        jp = L - 1 - tau
        wz_ref[jp * W:(jp + 1) * W, 0:STATES_PER_BLOCK] = m_re.astype(BF16)
        wz_ref[jp * W:(jp + 1) * W, STATES_PER_BLOCK:] = m_im.astype(BF16)
    zero = jnp.zeros((W, W), BF16)
    for jp in range(L):
        for j in range(L):
            tmat_ref[jp * W:(jp + 1) * W, j * W:(j + 1) * W] = kts[j - jp] if j >= jp else zero
    for j in range(L):
        qr, qi = pcr[j + 1], pci[j + 1]
        wc_ref[0:STATES_PER_BLOCK, j * W:(j + 1) * W] = (ctr * qr - cti * qi).astype(BF16)
        wc_ref[STATES_PER_BLOCK:, j * W:(j + 1) * W] = (-(ctr * qi + cti * qr)).astype(BF16)


def _blockdiag(t):
    a, b = t.shape[-2:]
    t = t.reshape(DEPTH, SSM_BLOCKS, GROUPS_PER_BLOCK, a, b)
    eye = jnp.eye(GROUPS_PER_BLOCK, dtype=t.dtype)
    out = jnp.einsum('lkgab,gh->lkgahb', t, eye)
    return out.reshape(DEPTH, SSM_BLOCKS, GROUPS_PER_BLOCK * a, GROUPS_PER_BLOCK * b)


def _ssm_prep(lam_re, lam_im, log_dt, b_re, b_im, c_re, c_im, d):
    S, KW = STATES_PER_BLOCK, SSM_CHUNK * LANES
    ldt = jnp.broadcast_to(log_dt[:, :, None], lam_re.shape)
    rows = [t.reshape(DEPTH, SSM_BLOCKS, 1, S) for t in (lam_re, lam_im, ldt)]
    cols = [t.reshape(DEPTH, SSM_BLOCKS, S, 1) for t in (lam_re, lam_im, ldt)]
    bt = [_blockdiag(jnp.swapaxes(t, -1, -2)) for t in (b_re, b_im)]
    ct = [_blockdiag(jnp.swapaxes(t, -1, -2)) for t in (c_re, c_im)]
    d4 = d.reshape(DEPTH, SSM_BLOCKS, 1, LANES)
    blk = lambda *s: pl.BlockSpec((None, None) + s, lambda l, k: (l, k) + (0,) * len(s))
    big = jax.ShapeDtypeStruct((DEPTH, SSM_BLOCKS, KW, KW), BF16)
    outs = pl.pallas_call(
        _ssm_prep_kernel,
        grid=(DEPTH, SSM_BLOCKS),
        in_specs=[blk(1, S)] * 3 + [blk(S, 1)] * 3 + [blk(LANES, S)] * 2 + [blk(S, LANES)] * 2
                 + [blk(1, LANES)],
        out_specs=[blk(KW, KW), blk(KW, KW), blk(KW, KW), blk(8, S), blk(2, LANES, S)],
        out_shape=[big, big, big,
                   jax.ShapeDtypeStruct((DEPTH, SSM_BLOCKS, 8, S), F32),
                   jax.ShapeDtypeStruct((DEPTH, SSM_BLOCKS, 2, LANES, S), F32)],
        compiler_params=_cparams(("arbitrary", "arbitrary")),
        name="ssm_prep",
    )(*rows, *cols, *bt, *ct, d4)
    tmat, wz, wc, apow, bbt = outs
    return dict(tmat=tmat, wz=wz, wc=wc, apow=apow, bbt=bbt, ct_re=ct[0], ct_im=ct[1], d4=d4)


SSM_SEQ_TILE = 2048
SSM_ROWS = SSM_SEQ_TILE // SSM_CHUNK


def _ssm_prompt_kernel(u_ref, tmat_ref, wz_ref, wc_ref, apow_ref, y_ref, st_ref,
                       up_scr, z_scr, h_scr, carry_scr):
    L, W, S, R = SSM_CHUNK, LANES, STATES_PER_BLOCK, SSM_ROWS

    @pl.when(pl.program_id(2) == 0)
    def _():
        carry_scr[...] = jnp.zeros_like(carry_scr)

    for j in range(L):
        up_scr[:, j * W:(j + 1) * W] = u_ref[pl.ds(j, R, stride=L), :].astype(BF16)
    up = up_scr[...]
    z_scr[...] = _dot(up, wz_ref[...])
    a8r = apow_ref[2:3, :]
    a8i = apow_ref[3:4, :]

    def step(m, carry):
        hr, hi = carry
        h_scr[pl.ds(m, 1), 0:S] = hr
        h_scr[pl.ds(m, 1), S:] = hi
        zr = z_scr[pl.ds(m, 1), 0:S]
        zi = z_scr[pl.ds(m, 1), S:]
        return a8r * hr - a8i * hi + zr, a8r * hi + a8i * hr + zi

    hr, hi = lax.fori_loop(0, R, step, (carry_scr[0:1, :], carry_scr[1:2, :]))
    carry_scr[0:1, :] = hr
    carry_scr[1:2, :] = hi
    st_ref[...] = jnp.concatenate([hr, hi, jnp.zeros((6, S), F32)], axis=0)
    y = _dot(up, tmat_ref[...]) + _dot(h_scr[...].astype(BF16), wc_ref[...])
    for j in range(L):
        y_ref[pl.ds(j, R, stride=L), :] = y[:, j * W:(j + 1) * W]


def _ssm_prompt(u, prep, l):
    bsz, seq, _ = u.shape
    KW, S = SSM_CHUNK * LANES, STATES_PER_BLOCK
    wspec = pl.BlockSpec((None, None, KW, KW), lambda k, b, s: (l, k, 0, 0))
    y, st = pl.pallas_call(
        _ssm_prompt_kernel,
        grid=(SSM_BLOCKS, bsz, seq // SSM_SEQ_TILE),
        in_specs=[
            pl.BlockSpec((None, SSM_SEQ_TILE, LANES), lambda k, b, s: (b, s, k)),
            wspec, wspec, wspec,
            pl.BlockSpec((None, None, 8, S), lambda k, b, s: (l, k, 0, 0)),
        ],
        out_specs=[
            pl.BlockSpec((None, SSM_SEQ_TILE, LANES), lambda k, b, s: (b, s, k)),
            pl.BlockSpec((None, None, 8, S), lambda k, b, s: (b, k, 0, 0)),
        ],
        out_shape=[jax.ShapeDtypeStruct((bsz, seq, SSM_WIDTH), F32),
                   jax.ShapeDtypeStruct((bsz, SSM_BLOCKS, 8, S), F32)],
        scratch_shapes=[pltpu.VMEM((SSM_ROWS, KW), BF16), pltpu.VMEM((SSM_ROWS, KW), F32),
                        pltpu.VMEM((SSM_ROWS, KW), F32), pltpu.VMEM((2, S), F32)],
        compiler_params=_cparams(("arbitrary", "arbitrary", "arbitrary")),
        name="ssm_prompt",
    )(u, prep['tmat'], prep['wz'], prep['wc'], prep['apow'])
    s_re = st[:, :, 0, :].reshape(bsz, SSM_GROUPS, SSM_STATE)
    s_im = st[:, :, 1, :].reshape(bsz, SSM_GROUPS, SSM_STATE)
    return y, s_re, s_im


def _ssm_sample_kernel(u_ref, sre_ref, sim_ref, apow_ref, bbt_ref, ctr_ref, cti_ref, d_ref,
                       y_ref, ore_ref, oim_ref):
    a_re = apow_ref[0:1, :]
    a_im = apow_ref[1:2, :]
    bb_re, bb_im = bbt_ref[0], bbt_ref[1]
    ctr, cti = ctr_ref[...], cti_ref[...]
    s_re, s_im = sre_ref[...], sim_ref[...]
    for j in range(u_ref.shape[0]):
        u = u_ref[j]
        n_re = a_re * s_re - a_im * s_im + _dot_x3(u, bb_re)
        n_im = a_re * s_im + a_im * s_re + _dot_x3(u, bb_im)
        s_re, s_im = n_re, n_im
        y_ref[j] = _dot_x3(s_re, ctr) - _dot_x3(s_im, cti) + d_ref[...] * u
    ore_ref[...] = s_re
    oim_ref[...] = s_im


def _ssm_sample(u, state_re, state_im, prep, l):
    t, db, _ = u.shape
    S = STATES_PER_BLOCK
    lk = lambda *s: pl.BlockSpec((None, None) + s, lambda k: (l, k) + (0,) * len(s))
    st_in = pl.BlockSpec((None, db, S), lambda k: (l, 0, k))
    st_out = pl.BlockSpec((db, S), lambda k: (0, k))
    useq = pl.BlockSpec((t, db, LANES), lambda k: (0, 0, k))
    return pl.pallas_call(
        _ssm_sample_kernel,
        grid=(SSM_BLOCKS,),
        in_specs=[useq, st_in, st_in, lk(8, S), lk(2, LANES, S), lk(S, LANES), lk(S, LANES), lk(1, LANES)],
        out_specs=[useq, st_out, st_out],
        out_shape=[jax.ShapeDtypeStruct((t, db, SSM_WIDTH), F32),
                   jax.ShapeDtypeStruct((db, SSM_GROUPS * SSM_STATE), F32),
                   jax.ShapeDtypeStruct((db, SSM_GROUPS * SSM_STATE), F32)],
        compiler_params=_cparams(("arbitrary",)),
        name="ssm_sample",
    )(u, state_re, state_im, prep['apow'], prep['bbt'], prep['ct_re'], prep['ct_im'], prep['d4'])


ATT_TILE = 512
SHIFT_LIMIT = 30.0
NEG_BIG = -1e30


def _attn_prompt_kernel(qi_ref, kj_ref, q_ref, k_ref, v_ref, fq_ref, fk_ref, o_ref,
                        qm_scr, fq_scr, m_scr, l_scr, acc_scr):
    pair = pl.program_id(1)
    step = pl.program_id(2)
    i = qi_ref[step]
    j = kj_ref[step]
    tq = q_ref.shape[0]
    lane = lax.broadcasted_iota(jnp.int32, (tq, LANES), 1)

    @pl.when(j == 0)
    def _():
        q = q_ref[...]
        fq = fq_ref[...]
        for hh in range(2):
            in_head = (lane >= HEAD_DIM * hh) & (lane < HEAD_DIM * (hh + 1))
            qm_scr[hh] = jnp.where(in_head, q, jnp.zeros_like(q))
            fq_scr[hh] = jnp.sum(jnp.where(lane == 2 * pair + hh, fq, 0.0), axis=1, keepdims=True)
            m_scr[hh] = jnp.full((tq, 1), NEG_BIG, F32)
            l_scr[hh] = jnp.zeros((tq, 1), F32)
            acc_scr[hh] = jnp.zeros((tq, LANES), F32)

    k = k_ref[...]
    v = v_ref[...]
    row = lax.broadcasted_iota(jnp.int32, (tq, tq), 0)
    col = lax.broadcasted_iota(jnp.int32, (tq, tq), 1)
    visible = (col <= row) | (j < i)
    for hh in range(2):
        fk = fk_ref[pl.ds(2 * pair + hh, 1), :]
        s = _dot_nt(qm_scr[hh], k) + (fq_scr[hh] - fk)
        s = jnp.where(visible, s, NEG_BIG)
        m_prev = m_scr[hh]
        m_new = jnp.maximum(m_prev, jnp.max(s, axis=1, keepdims=True))
        alpha = jnp.exp2(m_prev - m_new)
        p = jnp.exp2(s - m_new)
        l_scr[hh] = alpha * l_scr[hh] + jnp.sum(p, axis=1, keepdims=True)
        acc_scr[hh] = alpha * acc_scr[hh] + _dot(p.astype(BF16), v)
        m_scr[hh] = m_new

    @pl.when(j == i)
    def _():
        o_ref[...] = jnp.where(lane < HEAD_DIM, acc_scr[0] / l_scr[0], acc_scr[1] / l_scr[1])


def _tile_schedule(n):
    qi = np.concatenate([np.full(i + 1, i, np.int32) for i in range(n)])
    kj = np.concatenate([np.arange(i + 1, dtype=np.int32) for i in range(n)])
    return jnp.asarray(qi), jnp.asarray(kj)


def _attn_bounded_kernel(qi_ref, kj_ref, q_ref, k_ref, v_ref, fq_ref, o_ref, qa_scr, acc_scr):
    pair = pl.program_id(1)
    step = pl.program_id(2)
    i = qi_ref[step]
    j = kj_ref[step]
    tq = q_ref.shape[0]
    lane = lax.broadcasted_iota(jnp.int32, (tq, LANES), 1)

    @pl.when(j == 0)
    def _():
        q = q_ref[...]
        fq = fq_ref[...]
        for hh in range(2):
            in_head = (lane >= HEAD_DIM * hh) & (lane < HEAD_DIM * (hh + 1))
            qa_scr[hh, :, 0:LANES] = jnp.where(in_head, q, jnp.zeros_like(q))
            c = jnp.sum(jnp.where(lane == 2 * pair + hh, fq, 0.0), axis=1, keepdims=True)
            c_hi, c_mid, c_lo = [t.astype(F32) for t in _split3(c)]
            ones_at = (lane >= N_BIAS * (1 + hh)) & (lane < N_BIAS * (2 + hh))
            ext = jnp.where(lane == 0, c_hi, jnp.where(lane == 1, c_mid, jnp.where(lane == 2, c_lo,
                            jnp.where(ones_at, 1.0, 0.0))))
            qa_scr[hh, :, LANES:] = ext.astype(BF16)
            acc_scr[hh] = jnp.zeros(acc_scr.shape[1:], F32)

    def tile(causal):
        k = k_ref[...]
        v = v_ref[...]
        for hh in range(2):
            s = _dot_nt(qa_scr[hh], k)
            if causal:
                row = lax.broadcasted_iota(jnp.int32, (tq, tq), 0)
                col = lax.broadcasted_iota(jnp.int32, (tq, tq), 1)
                s = jnp.where(col <= row, s, NEG_BIG)
            acc_scr[hh] += _dot(jnp.exp2(s).astype(BF16), v)

    @pl.when(j < i)
    def _():
        tile(False)

    @pl.when(j == i)
    def _():
        tile(True)
        a0, a1 = acc_scr[0], acc_scr[1]
        o_ref[...] = jnp.where(lane < HEAD_DIM, a0[:, :LANES] / a0[:, LANES:LANES + 1],
                               a1[:, :LANES] / a1[:, LANES:LANES + 1])


def _attn_bounded(q, kcat, vcat, fq_shifted):
    bsz, seq, _ = q.shape
    t = ATT_TILE
    qi, kj = _tile_schedule(seq // t)
    qmap = lambda b, p, s, qi, kj: (b, qi[s], p)
    kmap = lambda b, p, s, qi, kj: (b, kj[s], p)
    grid_spec = pltpu.PrefetchScalarGridSpec(
        num_scalar_prefetch=2,
        grid=(bsz, ATT_HEADS // 2, qi.shape[0]),
        in_specs=[
            pl.BlockSpec((None, t, LANES), qmap),
            pl.BlockSpec((None, t, 2 * LANES), kmap),
            pl.BlockSpec((None, t, 2 * LANES), kmap),
            pl.BlockSpec((None, t, LANES), lambda b, p, s, qi, kj: (b, qi[s], 0)),
        ],
        out_specs=pl.BlockSpec((None, t, LANES), qmap),
        scratch_shapes=[pltpu.VMEM((2, t, 2 * LANES), BF16), pltpu.VMEM((2, t, 2 * LANES), F32)],
    )
    return pl.pallas_call(
        _attn_bounded_kernel,
        grid_spec=grid_spec,
        out_shape=jax.ShapeDtypeStruct((bsz, seq, ATT_WIDTH), F32),
        compiler_params=_cparams(("arbitrary", "arbitrary", "arbitrary")),
        name="attn_bounded",
    )(qi, kj, q, kcat, vcat, fq_shifted)


def _attn_prompt(q, kcat, vcat, fcum, fcum_t):
    bsz, seq, _ = q.shape
    t = ATT_TILE
    qi, kj = _tile_schedule(seq // t)
    qmap = lambda b, p, s, qi, kj: (b, qi[s], p)
    kmap = lambda b, p, s, qi, kj: (b, kj[s], 2 * p)
    grid_spec = pltpu.PrefetchScalarGridSpec(
        num_scalar_prefetch=2,
        grid=(bsz, ATT_HEADS // 2, qi.shape[0]),
        in_specs=[
            pl.BlockSpec((None, t, LANES), qmap),
            pl.BlockSpec((None, t, LANES), kmap),
            pl.BlockSpec((None, t, LANES), kmap),
            pl.BlockSpec((None, t, LANES), lambda b, p, s, qi, kj: (b, qi[s], 0)),
            pl.BlockSpec((None, ATT_HEADS, t), lambda b, p, s, qi, kj: (b, 0, kj[s])),
        ],
        out_specs=pl.BlockSpec((None, t, LANES), qmap),
        scratch_shapes=[pltpu.VMEM((2, t, LANES), BF16), pltpu.VMEM((2, t, 1), F32),
                        pltpu.VMEM((2, t, 1), F32), pltpu.VMEM((2, t, 1), F32),
                        pltpu.VMEM((2, t, LANES), F32)],
    )
    return pl.pallas_call(
        _attn_prompt_kernel,
        grid_spec=grid_spec,
        out_shape=jax.ShapeDtypeStruct((bsz, seq, ATT_WIDTH), F32),
        compiler_params=_cparams(("arbitrary", "arbitrary", "arbitrary")),
        name="attn_prompt",
    )(qi, kj, q, kcat, vcat, fcum, fcum_t)


PAGES_PER_STEP = 8
FLAT = PAGE_SIZE * ATT_HEADS


FPAST_ROWS = 512
FPAST_POOL = 512


def _fpast_mats(n_pages):
    src = np.arange(FLAT)
    dst = np.arange(FLAT)
    same = (src[:, None] % ATT_HEADS) == (dst[None, :] // PAGE_SIZE)
    within = same & ((src[:, None] // ATT_HEADS) <= (dst[None, :] % PAGE_SIZE))
    r = np.arange(FPAST_ROWS)
    before = ((r[:, None] // n_pages) == (r[None, :] // n_pages)) & ((r[None, :] % n_pages) < (r[:, None] % n_pages))
    to_bf = lambda m: jnp.asarray(np.asarray(m, np.float32), dtype=BF16)
    return to_bf(within), to_bf(same), to_bf(before)


def _fpast_kernel(pt_ref, lf_ref, within_ref, same_ref, before_ref, f_ref, x_scr):
    c = pl.program_id(1)

    @pl.when(c == 0)
    def _():
        x_scr[...] = jnp.zeros_like(x_scr)

    def times(parts, m):
        return _dot(parts[0], m) + (_dot(parts[1], m) + _dot(parts[2], m))

    pool = lax.broadcasted_iota(jnp.int32, (FPAST_ROWS, FPAST_POOL), 1) + c * FPAST_POOL
    sel = jnp.where(pool == pt_ref[...], 1.0, 0.0).astype(BF16)
    hi, mid, lo = _split3(lf_ref[...])
    x_scr[...] += _dot(sel, hi) + (_dot(sel, mid) + _dot(sel, lo))

    @pl.when(c == pl.num_programs(1) - 1)
    def _():
        parts = _split3(x_scr[...])
        in_page = times(parts, within_ref[...])
        totals = _split3(times(parts, same_ref[...]))
        before = before_ref[...]
        f_ref[...] = in_page + (_dot(before, totals[0]) + (_dot(before, totals[1]) + _dot(before, totals[2])))


def _fpast(logf_view, page_table, l):
    db, n_pages = page_table.shape
    n_pool = logf_view.shape[1]
    rows = db * n_pages
    assert rows % FPAST_ROWS == 0 and FPAST_ROWS % n_pages == 0 and n_pool % FPAST_POOL == 0
    within, same, before = _fpast_mats(n_pages)
    const = lambda g, c: (0, 0)
    return pl.pallas_call(
        _fpast_kernel,
        grid=(rows // FPAST_ROWS, n_pool // FPAST_POOL),
        in_specs=[pl.BlockSpec((FPAST_ROWS, 1), lambda g, c: (g, 0)),
                  pl.BlockSpec((None, FPAST_POOL, FLAT), lambda g, c: (l, c, 0)),
                  pl.BlockSpec((FLAT, FLAT), const), pl.BlockSpec((FLAT, FLAT), const),
                  pl.BlockSpec((FPAST_ROWS, FPAST_ROWS), const)],
        out_specs=pl.BlockSpec((FPAST_ROWS, FLAT), lambda g, c: (g, 0)),
        out_shape=jax.ShapeDtypeStruct((rows, FLAT), F32),
        scratch_shapes=[pltpu.VMEM((FPAST_ROWS, FLAT), F32)],
        compiler_params=_cparams(("arbitrary", "arbitrary")),
        name="fpast",
    )(page_table.reshape(rows, 1), logf_view, within, same, before)


def _attn_sample_kernel(pt_ref, *refs):
    n = PAGES_PER_STEP
    k_refs, v_refs = refs[:n], refs[n:2 * n]
    (f_ref, tot_ref, qblk_ref, q_ref, kn_ref, vn_ref, lfn_ref, bmask_ref, nmask_ref, ncum_ref,
     o_ref, m_scr, l_scr, acc_scr) = refs[2 * n:]
    c = pl.program_id(1)
    nq = q_ref.shape[0]
    t_new = nq // ATT_HEADS

    @pl.when(c == 0)
    def _():
        m_scr[...] = jnp.full(m_scr.shape, NEG_BIG, F32)
        l_scr[...] = jnp.zeros_like(l_scr)
        acc_scr[...] = jnp.zeros_like(acc_scr)

    qblk = qblk_ref[...]
    scores = []
    for pg in range(n):
        kt = k_refs[pg][...].reshape(ATT_WIDTH, PAGE_SIZE).astype(BF16)
        f = f_ref[pg]
        scores.append(_dot(qblk, kt) - jnp.concatenate([f] * t_new, axis=0))
    m_blk = functools.reduce(jnp.maximum, scores)
    m_prev = m_scr[...]
    m_new = jnp.maximum(m_prev, jnp.max(m_blk, axis=1, keepdims=True))
    alpha = jnp.exp(m_prev - m_new)
    l_new = alpha * l_scr[...]
    acc = alpha * acc_scr[...]
    for pg in range(n):
        p = jnp.exp(scores[pg] - m_new)
        l_new = l_new + jnp.sum(p, axis=1, keepdims=True)
        vt = v_refs[pg][...].reshape(ATT_WIDTH, PAGE_SIZE).astype(BF16)
        acc = acc + _dot_nt(p.astype(BF16), vt)
    m_scr[...] = m_new
    l_scr[...] = l_new
    acc_scr[...] = acc

    @pl.when(c == pl.num_programs(1) - 1)
    def _():
        own = acc_scr[...] * bmask_ref[...]
        past = own[:, 0:HEAD_DIM]
        for h in range(1, ATT_HEADS):
            past = past + own[:, h * HEAD_DIM:(h + 1) * HEAD_DIM]
        hi, mid, lo = _split3(lfn_ref[...])
        ncum = ncum_ref[...]
        fnew = _dot(hi, ncum) + (_dot(mid, ncum) + _dot(lo, ncum)) + tot_ref[...]
        kn = kn_ref[...].astype(BF16)
        s = _dot_nt(q_ref[...], kn) - fnew + nmask_ref[...]
        m_prev = m_scr[...]
        m_new = jnp.maximum(m_prev, jnp.max(s, axis=1, keepdims=True))
        alpha = jnp.exp(m_prev - m_new)
        p = jnp.exp(s - m_new)
        l_fin = alpha * l_scr[...] + jnp.sum(p, axis=1, keepdims=True)
        o_ref[...] = (alpha * past + _dot(p.astype(BF16), vn_ref[...].astype(BF16))) / l_fin


def _attn_sample(q32, kn32, vn32, lfn, cache_kt, cache_vt, fpast, ftot, page_table, l):
    db, n_pages = page_table.shape
    n = PAGES_PER_STEP
    nq = q32.shape[1]
    r = np.arange(nq)
    col = np.arange(ATT_WIDTH)
    own = (col[None, :] // HEAD_DIM) == (r[:, None] % ATT_HEADS)
    qblk = jnp.where(jnp.asarray(own)[None], jnp.tile(q32, (1, 1, ATT_HEADS)), jnp.zeros((), q32.dtype))
    ok = ((r[None, :] % ATT_HEADS) == (r[:, None] % ATT_HEADS)) & ((r[None, :] // ATT_HEADS) <= (r[:, None] // ATT_HEADS))
    nmask = np.where(ok, 0.0, NEG_BIG).astype(np.float32)
    ncum = (((r[:, None] % ATT_HEADS) == (r[None, :] % ATT_HEADS)) & (r[:, None] <= r[None, :])).astype(np.float32)
    pmap = lambda i: (lambda b, c, pt: (l, pt[b * n_pages + c * n + i], 0, 0, 0))
    bmap = lambda b, c, pt: (b, 0, 0)
    const = lambda b, c, pt: (0, 0)
    page = lambda i: pl.BlockSpec((None, None, ATT_HEADS, HEAD_DIM, PAGE_SIZE), pmap(i))
    grid_spec = pltpu.PrefetchScalarGridSpec(
        num_scalar_prefetch=1,
        grid=(db, n_pages // n),
        in_specs=[page(i) for i in range(n)] + [page(i) for i in range(n)] + [
            pl.BlockSpec((None, n, 8, LANES), lambda b, c, pt: (b, c, 0, 0)),
            pl.BlockSpec((None, 1, nq), bmap),
            pl.BlockSpec((None, nq, ATT_WIDTH), bmap),
            pl.BlockSpec((None, nq, HEAD_DIM), bmap),
            pl.BlockSpec((None, nq, HEAD_DIM), bmap),
            pl.BlockSpec((None, nq, HEAD_DIM), bmap),
            pl.BlockSpec((None, 1, nq), bmap),
            pl.BlockSpec((nq, ATT_WIDTH), const),
            pl.BlockSpec((nq, nq), const),
            pl.BlockSpec((nq, nq), const),
        ],
        out_specs=pl.BlockSpec((None, nq, HEAD_DIM), bmap),
        scratch_shapes=[pltpu.VMEM((nq, 1), F32), pltpu.VMEM((nq, 1), F32), pltpu.VMEM((nq, ATT_WIDTH), F32)],
    )
    return pl.pallas_call(
        _attn_sample_kernel,
        grid_spec=grid_spec,
        out_shape=jax.ShapeDtypeStruct((db, nq, HEAD_DIM), F32),
        compiler_params=_cparams(("arbitrary", "arbitrary")),
        name="attn_sample",
    )(page_table.reshape(-1), *([cache_kt] * n), *([cache_vt] * n), fpast, ftot, qblk, q32, kn32, vn32, lfn,
      jnp.asarray(own.astype(np.float32)), jnp.asarray(nmask), jnp.asarray(ncum, dtype=BF16))


def _merge_kernel(ys_ref, ya_ref, x_ref, gate1_ref, shift2_ref, scale2_ref, gs_ref, ga_ref, g2_ref,
                  wglu_ref, bglu_ref, wos_ref, woa_ref, wr_ref, br_ref, cnt_in_ref, strict_ref,
                  x1_ref, h2_ref, eidx_ref, gate_ref, rank_ref, cnt_out_ref, cnt_scr, *, hi_prec):
    first = (pl.program_id(0) == 0) & (pl.program_id(1) == 0)

    @pl.when(first)
    def _():
        cnt_scr[...] = cnt_in_ref[...]

    mm = _dot_x3 if hi_prec else (lambda a, b: _dot(a.astype(BF16), b))
    y = jax.nn.gelu(ys_ref[...])
    y = y * jax.nn.sigmoid(mm(y, wglu_ref[...]) + bglu_ref[...])
    mix = mm(_rms(y, gs_ref[...]), wos_ref[...]) + mm(_rms(ya_ref[...], ga_ref[...]), woa_ref[...])
    x1 = x_ref[...] + gate1_ref[...] * mix
    x1_ref[...] = x1
    h2 = _rms(x1, g2_ref[...]) * (1.0 + scale2_ref[...]) + shift2_ref[...]
    h2_ref[...] = h2
    logits = _dot_x3(h2, wr_ref[...]) + br_ref[...]
    rows = logits.shape[0]
    lane = lax.broadcasted_iota(jnp.int32, (rows, LANES), 1)
    lane_f = lane.astype(F32)
    vals, idxs = [], []
    for _ in range(TOP_K):
        m = jnp.max(logits, axis=1, keepdims=True)
        idx = jnp.min(jnp.where(logits == m, lane_f, float(LANES)), axis=1, keepdims=True)
        vals.append(m)
        idxs.append(idx)
        logits = jnp.where(lane_f == idx, -jnp.inf, logits)
    exps = [jnp.exp(v - vals[0]) for v in vals]
    den = exps[0] + exps[1] + exps[2] + exps[3]
    cnt = cnt_scr[...]
    strict = strict_ref[...]
    e_out = jnp.zeros((rows, LANES), F32)
    g_out = jnp.zeros((rows, LANES), F32)
    r_out = jnp.zeros((rows, LANES), F32)
    for kk in range(TOP_K):
        onehot = lane_f == idxs[kk]
        oh = jnp.where(onehot, 1.0, 0.0)
        before = _dot(strict, oh.astype(BF16)) + cnt
        rank = jnp.sum(jnp.where(onehot, before, 0.0), axis=1, keepdims=True)
        cnt = cnt + jnp.sum(oh, axis=0, keepdims=True)
        e_out = jnp.where(lane == kk, idxs[kk], e_out)
        g_out = jnp.where(lane == kk, exps[kk] / den, g_out)
        r_out = jnp.where(lane == kk, rank, r_out)
    cnt_scr[...] = cnt
    cnt_out_ref[...] = cnt
    eidx_ref[...] = e_out.astype(jnp.int32)
    gate_ref[...] = g_out
    rank_ref[...] = r_out.astype(jnp.int32)


def _merge(ys, ya, x, gate1, shift2, scale2, g_out_ssm, g_out_att, g2, w_glu, b_glu, w_out_s, w_out_a,
           w_router, b_router, cnt_in, *, tile, hi_prec):
    s_dim, r_dim, _ = x.shape
    rm = gate1.shape[1]
    row = lambda s, t: (s, t, 0)
    const2 = lambda s, t: (0, 0)
    mod_map = (lambda s, t: (s, 0, 0)) if rm == 1 else row
    mod_blk = (None, 1, D_MODEL) if rm == 1 else (None, tile, D_MODEL)
    full = lambda a: pl.BlockSpec(a.shape, const2)
    half = pl.BlockSpec((None, tile, SSM_WIDTH), row)
    wide = pl.BlockSpec((None, tile, D_MODEL), row)
    narrow = pl.BlockSpec((None, tile, LANES), row)
    strict = _tri_matrix(tile, strict=True)
    args = (ys, ya, x, gate1, shift2, scale2, g_out_ssm, g_out_att, g2, w_glu, b_glu, w_out_s, w_out_a,
            w_router, b_router, cnt_in, strict)
    return pl.pallas_call(
        functools.partial(_merge_kernel, hi_prec=hi_prec),
        grid=(s_dim, r_dim // tile),
        in_specs=[half, half, wide, pl.BlockSpec(mod_blk, mod_map), pl.BlockSpec(mod_blk, mod_map),
                  pl.BlockSpec(mod_blk, mod_map)] + [full(a) for a in args[6:]],
        out_specs=[wide, wide, narrow, narrow, narrow, pl.BlockSpec((1, LANES), const2)],
        out_shape=[jax.ShapeDtypeStruct((s_dim, r_dim, D_MODEL), F32),
                   jax.ShapeDtypeStruct((s_dim, r_dim, D_MODEL), F32),
                   jax.ShapeDtypeStruct((s_dim, r_dim, LANES), jnp.int32),
                   jax.ShapeDtypeStruct((s_dim, r_dim, LANES), F32),
                   jax.ShapeDtypeStruct((s_dim, r_dim, LANES), jnp.int32),
                   jax.ShapeDtypeStruct((1, LANES), F32)],
        scratch_shapes=[pltpu.VMEM((1, LANES), F32)],
        compiler_params=_cparams(("arbitrary", "arbitrary")),
        name="merge_hi" if hi_prec else "merge",
    )(*args)


MOE_TILE = 256
ROW_TILE = 128
DMA_UNROLL = 8


def _dispatch_kernel(fill_ref, dest_ref, hp_ref, hs_ref, xpad_ref, zero_scr, sem):
    i = pl.program_id(0)
    last = pl.num_programs(0) - 1
    rows = hp_ref.shape[0]

    def scatter(h_ref):
        def row_copy(src_row, dst_row):
            return pltpu.make_async_copy(h_ref.at[pl.ds(src_row, 1)], xpad_ref.at[pl.ds(dst_row, 1)], sem)

        def issue(t, carry):
            for kk in range(TOP_K):
                row_copy(t, dest_ref[0, t * TOP_K + kk]).start()
            return carry

        lax.fori_loop(0, rows, issue, 0, unroll=DMA_UNROLL // TOP_K)

        def drain(a, carry):
            row_copy(0, 0).wait()
            return carry

        lax.fori_loop(0, rows * TOP_K, drain, 0, unroll=DMA_UNROLL)

    @pl.when(i < last)
    def _():
        scatter(hp_ref)

    @pl.when(i == last)
    def _():
        scatter(hs_ref)
        zero_scr[...] = jnp.zeros_like(zero_scr)

        def zero_row(dst_row):
            return pltpu.make_async_copy(zero_scr.at[pl.ds(0, 1)], xpad_ref.at[pl.ds(dst_row, 1)], sem)

        def per_expert(e, carry):
            start, count = fill_ref[2 * e], fill_ref[2 * e + 1]

            def fill(a, c2):
                zero_row(start + a).start()
                return c2

            lax.fori_loop(0, count, fill, 0)

            def fill_wait(a, c2):
                zero_row(0).wait()
                return c2

            lax.fori_loop(0, count, fill_wait, 0)
            return carry

        lax.fori_loop(0, N_EXPERTS, per_expert, 0)

        def zero_tile(t):
            return pltpu.make_async_copy(zero_scr, xpad_ref.at[pl.ds(t * MOE_TILE, MOE_TILE)], sem)

        n_used = fill_ref[2 * N_EXPERTS]
        n_tiles = xpad_ref.shape[0] // MOE_TILE

        def tail(t, carry):
            zero_tile(t).start()
            return carry

        lax.fori_loop(n_used, n_tiles, tail, 0)

        def tail_wait(t, carry):
            zero_tile(0).wait()
            return carry

        lax.fori_loop(n_used, n_tiles, tail_wait, 0)


def _dispatch(h2p, h2s, dest, fill, cap):
    ntp = h2p.shape[0] // ROW_TILE
    nt = ntp + 1
    grid_spec = pltpu.PrefetchScalarGridSpec(
        num_scalar_prefetch=1,
        grid=(nt,),
        in_specs=[pl.BlockSpec((None, 1, ROW_TILE * TOP_K), lambda i, f: (i, 0, 0), memory_space=pltpu.SMEM),
                  pl.BlockSpec((ROW_TILE, D_MODEL), lambda i, f: (jnp.minimum(i, ntp - 1), 0)),
                  pl.BlockSpec((ROW_TILE, D_MODEL), lambda i, f: (0, 0))],
        out_specs=pl.BlockSpec(memory_space=pl.ANY),
        scratch_shapes=[pltpu.VMEM((MOE_TILE, D_MODEL), F32), pltpu.SemaphoreType.DMA(())],
    )
    return pl.pallas_call(
        _dispatch_kernel,
        grid_spec=grid_spec,
        out_shape=jax.ShapeDtypeStruct((cap, D_MODEL), F32),
        compiler_params=_cparams(("arbitrary",)),
        name="dispatch",
    )(fill, dest.reshape(nt, 1, ROW_TILE * TOP_K), h2p, h2s)


def _experts_kernel(te_ref, nu_ref, x_ref, wgu_ref, bgu_ref, wd_ref, bd_ref, y_ref, wgu_scr, wd_scr):
    i = pl.program_id(0)
    prev = te_ref[jnp.maximum(i - 1, 0)]
    fresh = (i == 0) | (te_ref[i] != prev)

    @pl.when(fresh)
    def _():
        wgu_scr[...] = wgu_ref[...].astype(BF16)
        wd_scr[...] = wd_ref[...].astype(BF16)

    @pl.when(i < nu_ref[0])
    def _():
        gu = _dot(x_ref[...].astype(BF16), wgu_scr[...]) + bgu_ref[...]
        g = jnp.minimum(gu[:, :D_EXPERT], SWIGLU_LIMIT)
        up = jnp.clip(gu[:, D_EXPERT:], -SWIGLU_LIMIT, SWIGLU_LIMIT)
        act = (up + 1.0) * (g * jax.nn.sigmoid(SWIGLU_ALPHA * g))
        y_ref[...] = _dot(act.astype(BF16), wd_scr[...]) + bd_ref[...]

    @pl.when(i >= nu_ref[0])
    def _():
        y_ref[...] = jnp.zeros_like(y_ref)


def _experts(xpad, tile_expert, n_used, w_gate_up, b_gate_up, w_down, b_down, l):
    cap = xpad.shape[0]
    n_tiles = cap // MOE_TILE
    tmap = lambda i, te, nu: (jnp.minimum(i, nu[0] - 1), 0)
    grid_spec = pltpu.PrefetchScalarGridSpec(
        num_scalar_prefetch=2,
        grid=(n_tiles,),
        in_specs=[
            pl.BlockSpec((MOE_TILE, D_MODEL), tmap),
            pl.BlockSpec((None, None, D_MODEL, 2 * D_EXPERT), lambda i, te, nu: (l, te[i], 0, 0)),
            pl.BlockSpec((None, None, 1, 2 * D_EXPERT), lambda i, te, nu: (l, te[i], 0, 0)),
            pl.BlockSpec((None, None, D_EXPERT, D_MODEL), lambda i, te, nu: (l, te[i], 0, 0)),
            pl.BlockSpec((None, None, 1, D_MODEL), lambda i, te, nu: (l, te[i], 0, 0)),
        ],
        out_specs=pl.BlockSpec((MOE_TILE, D_MODEL), lambda i, te, nu: (i, 0)),
        scratch_shapes=[pltpu.VMEM((D_MODEL, 2 * D_EXPERT), BF16), pltpu.VMEM((D_EXPERT, D_MODEL), BF16)],
    )
    return pl.pallas_call(
        _experts_kernel,
        grid_spec=grid_spec,
        out_shape=jax.ShapeDtypeStruct((cap, D_MODEL), F32),
        compiler_params=_cparams(("arbitrary",)),
        name="experts",
    )(tile_expert, n_used, xpad, w_gate_up, b_gate_up.reshape(DEPTH, N_EXPERTS, 1, 2 * D_EXPERT),
      w_down, b_down.reshape(DEPTH, N_EXPERTS, 1, D_MODEL))


def _combine_kernel(dest_ref, x1_ref, gate2_ref, g_ref, ypad_ref, o_ref, buf, sem):
    rows = x1_ref.shape[0]

    def row_copy(src_row, slot):
        return pltpu.make_async_copy(ypad_ref.at[pl.ds(src_row, 1)], buf.at[pl.ds(slot, 1)], sem)

    def issue(t, carry):
        for kk in range(TOP_K):
            row_copy(dest_ref[0, t * TOP_K + kk], kk * rows + t).start()
        return carry

    lax.fori_loop(0, rows, issue, 0, unroll=DMA_UNROLL // TOP_K)

    def drain(a, carry):
        row_copy(0, 0).wait()
        return carry

    lax.fori_loop(0, rows * TOP_K, drain, 0, unroll=DMA_UNROLL)
    g = g_ref[...]
    moe = jnp.zeros((rows, D_MODEL), F32)
    for kk in range(TOP_K):
        moe = moe + g[:, kk:kk + 1] * buf[kk * rows:(kk + 1) * rows, :]
    o_ref[...] = x1_ref[...] + gate2_ref[...] * moe


def _combine(x1, gate2, gates, dest, ypad):
    s_dim, r_dim, _ = x1.shape
    rm = gate2.shape[1]
    nt = r_dim // ROW_TILE
    row = lambda s, t: (s, t, 0)
    mod_map = (lambda s, t: (s, 0, 0)) if rm == 1 else row
    mod_blk = (None, 1, D_MODEL) if rm == 1 else (None, ROW_TILE, D_MODEL)
    return pl.pallas_call(
        _combine_kernel,
        grid=(s_dim, nt),
        in_specs=[pl.BlockSpec((None, 1, ROW_TILE * TOP_K), lambda s, t: (s * nt + t, 0, 0),
                               memory_space=pltpu.SMEM),
                  pl.BlockSpec((None, ROW_TILE, D_MODEL), row), pl.BlockSpec(mod_blk, mod_map),
                  pl.BlockSpec((None, ROW_TILE, LANES), row), pl.BlockSpec(memory_space=pl.ANY)],
        out_specs=pl.BlockSpec((None, ROW_TILE, D_MODEL), row),
        out_shape=jax.ShapeDtypeStruct(x1.shape, F32),
        scratch_shapes=[pltpu.VMEM((TOP_K * ROW_TILE, D_MODEL), F32), pltpu.SemaphoreType.DMA(())],
        compiler_params=_cparams(("arbitrary", "arbitrary")),
        name="combine",
    )(dest.reshape(s_dim * nt, 1, ROW_TILE * TOP_K), x1, gate2, gates, ypad)


def _mods(mod_rows):
    return [m for m in jnp.split(mod_rows, N_MOD, axis=-1)]


def kernel(x_prompt, x_sample, c_prompt, c_sample, cache_k, cache_v, cache_logf, state_ssm_re, state_ssm_im,
           page_table, w_ada, b_ada, g_norm1, g_norm2, w_in, b_forget, g_q, g_k, ssm_lam_re, ssm_lam_im,
           ssm_log_dt, ssm_b_re, ssm_b_im, ssm_c_re, ssm_c_im, ssm_d, w_glu, b_glu, g_out_ssm, g_out_att, w_out,
           w_router, b_router, w_gate_up, b_gate_up, w_down, b_down):
    bsz, seq, _ = x_prompt.shape
    db, t_new, _ = x_sample.shape
    n_pool = cache_k.shape[1]
    n_seq = bsz + db
    c_rows = -(-n_seq // 8) * 8
    c_all = jnp.concatenate([c_prompt, c_sample, jnp.zeros((c_rows - n_seq, D_MODEL), F32)], axis=0)
    mod = _adaln(c_all, w_ada, b_ada)
    prep = _ssm_prep(ssm_lam_re, ssm_lam_im, ssm_log_dt, ssm_b_re, ssm_b_im, ssm_c_re, ssm_c_im, ssm_d)
    logf_view = cache_logf.reshape(DEPTH, n_pool, FLAT)
    cache_kt = jnp.transpose(cache_k, (0, 1, 3, 4, 2))
    cache_vt = jnp.transpose(cache_v, (0, 1, 3, 4, 2))
    n_pages = page_table.shape[1]
    st_re = state_ssm_re.reshape(DEPTH, db, SSM_GROUPS * SSM_STATE)
    st_im = state_ssm_im.reshape(DEPTH, db, SSM_GROUPS * SSM_STATE)
    n_tok_p, n_tok_s = bsz * seq, db * t_new
    assert n_tok_s == ROW_TILE and n_tok_p % ROW_TILE == 0
    n_tiles = -(-(n_tok_p + n_tok_s) * TOP_K // MOE_TILE) + N_EXPERTS
    cap = n_tiles * MOE_TILE
    pad_f = LANES - ATT_HEADS
    pad_e = LANES - N_EXPERTS

    xp = x_prompt
    xs = jnp.swapaxes(x_sample, 0, 1)
    outs = {name: [] for name in ('kp', 'vp', 'lfp', 'srp', 'sip', 'ks', 'vs', 'lfs', 'srs', 'sis')}
    for l in range(DEPTH):
        mp = [m[:, None, :] for m in _mods(mod[l, :bsz])]
        ms = [jnp.broadcast_to(m[None], (t_new, db, D_MODEL)) for m in _mods(mod[l, bsz:n_seq])]
        w_main = w_in[l, :, :IN_MAIN]
        w_f = jnp.pad(w_in[l, :, IN_MAIN:], ((0, 0), (0, pad_f)))
        b_f = jnp.pad(b_forget[l], (0, pad_f))[None]
        gq = jnp.tile(g_q[l], ATT_HEADS)[None]
        gk = jnp.tile(g_k[l], ATT_HEADS)[None]
        g1 = g_norm1[l][None]
        g2 = g_norm2[l][None]
        w_os, w_oa = w_out[l, :SSM_WIDTH], w_out[l, SSM_WIDTH:]
        w_r = jnp.pad(w_router[l], ((0, 0), (0, pad_e)))
        b_r = jnp.pad(b_router[l], (0, pad_e), constant_values=NEG_BIG)[None]
        small = (g_out_ssm[l][None], g_out_att[l][None], g2)

        u, q, k, v, kcat, vcat, lf, fc = _in_proj(xp, mp[0], mp[1], g1, w_main.astype(BF16), w_f.astype(BF16),
                                                  b_f, gq, gk, tile=512, hi_prec=False, with_cumsum=True)
        y_ssm, s_re, s_im = _ssm_prompt(u, prep, l)
        bound = float(HEAD_DIM) * ATT_SCALE * jnp.max(jnp.abs(g_q[l])) * jnp.max(jnp.abs(g_k[l]))
        y_att = lax.cond(
            bound <= SHIFT_LIMIT,
            lambda: _attn_bounded(q, kcat, vcat, fc - bound * LOG2E),
            lambda: _attn_prompt(q, kcat, vcat, fc, jnp.swapaxes(fc[..., :ATT_HEADS], 1, 2)))
        x1p, h2p, e_p, gate_p, rank_p, cnt_p = _merge(
            y_ssm, y_att, xp, mp[2], mp[3], mp[4], *small, w_glu[l].astype(BF16), b_glu[l][None],
            w_os.astype(BF16), w_oa.astype(BF16), w_r, b_r, jnp.zeros((1, LANES), F32), tile=512, hi_prec=False)
        outs['kp'].append(k.reshape(bsz, seq, ATT_HEADS, HEAD_DIM))
        outs['vp'].append(v.reshape(bsz, seq, ATT_HEADS, HEAD_DIM))
        outs['lfp'].append(lf[..., :ATT_HEADS])
        outs['srp'].append(s_re)
        outs['sip'].append(s_im)

        u_s, q_s, k_s, v_s, _, _, lf_s, _ = _in_proj(xs, ms[0], ms[1], g1, w_main, w_f, b_f, gq, gk,
                                                     tile=db, hi_prec=True, with_cumsum=False)
        y_ssm_s, o_re, o_im = _ssm_sample(u_s, st_re, st_im, prep, l)
        per_seq = lambda a: jnp.swapaxes(a, 0, 1).reshape(db, t_new * ATT_HEADS, HEAD_DIM)
        lfn = jnp.swapaxes(lf_s[..., :ATT_HEADS], 0, 1).reshape(db, 1, t_new * ATT_HEADS)
        fpast = _fpast(logf_view, page_table, l).reshape(db, n_pages, ATT_HEADS, PAGE_SIZE)
        ftot = jnp.tile(fpast[:, -1, :, -1], (1, t_new)).reshape(db, 1, t_new * ATT_HEADS)
        o_s = _attn_sample(per_seq(q_s), per_seq(k_s), per_seq(v_s), lfn, cache_kt, cache_vt, fpast, ftot,
                           page_table, l)
        y_att_s = jnp.swapaxes(o_s.reshape(db, t_new, ATT_WIDTH), 0, 1)
        x1s, h2s, e_s, gate_s, rank_s, cnt = _merge(
            y_ssm_s, y_att_s, xs, ms[2], ms[3], ms[4], *small, w_glu[l], b_glu[l][None], w_os, w_oa, w_r, b_r,
            cnt_p, tile=db, hi_prec=True)
        outs['ks'].append(jnp.swapaxes(k_s, 0, 1).reshape(db, t_new, ATT_HEADS, HEAD_DIM))
        outs['vs'].append(jnp.swapaxes(v_s, 0, 1).reshape(db, t_new, ATT_HEADS, HEAD_DIM))
        outs['lfs'].append(jnp.swapaxes(lf_s[..., :ATT_HEADS], 0, 1))
        outs['srs'].append(o_re.reshape(db, SSM_GROUPS, SSM_STATE))
        outs['sis'].append(o_im.reshape(db, SSM_GROUPS, SSM_STATE))

        counts = cnt[0, :N_EXPERTS].astype(jnp.int32)
        padded = (counts + MOE_TILE - 1) // MOE_TILE * MOE_TILE
        pend = jnp.cumsum(padded)
        pstart = pend - padded
        experts = jnp.arange(N_EXPERTS, dtype=jnp.int32)

        def slots(e, rank):
            hit = e[..., :TOP_K, None] == experts
            return (jnp.sum(jnp.where(hit, pstart, 0), axis=-1) + rank[..., :TOP_K]).reshape(-1)

        dest_p = slots(e_p, rank_p)
        dest_s = slots(e_s, rank_s)
        tile_start = jnp.arange(n_tiles, dtype=jnp.int32) * MOE_TILE
        tile_expert = jnp.sum((pend[None, :] <= tile_start[:, None]).astype(jnp.int32), axis=1)
        tile_expert = jnp.minimum(tile_expert, N_EXPERTS - 1)
        n_used = (pend[-1:] // MOE_TILE).astype(jnp.int32)
        fill = jnp.stack([pstart + counts, padded - counts], axis=-1).reshape(-1)
        fill = jnp.concatenate([fill, n_used]).astype(jnp.int32)
        xpad = _dispatch(h2p.reshape(n_tok_p, D_MODEL), h2s.reshape(n_tok_s, D_MODEL),
                         jnp.concatenate([dest_p, dest_s]), fill, cap)
        ypad = _experts(xpad, tile_expert, n_used, w_gate_up, b_gate_up, w_down, b_down, l)
        xp = _combine(x1p, mp[5], gate_p, dest_p, ypad)
        xs = _combine(x1s.reshape(1, n_tok_s, D_MODEL), ms[5].reshape(1, n_tok_s, D_MODEL),
                      gate_s.reshape(1, n_tok_s, LANES), dest_s, ypad).reshape(t_new, db, D_MODEL)

    st = lambda name: jnp.stack(outs[name])
    return (xp, jnp.swapaxes(xs, 0, 1), st('kp'), st('vp'), st('lfp'), st('srp'), st('sip'),
            st('ks'), st('vs'), st('lfs'), st('srs'), st('sis'))
```

```python
import functools
import math

import jax
import jax.numpy as jnp
import numpy as np
from jax import lax
from jax.experimental import pallas as pl
from jax.experimental.pallas import tpu as pltpu

F32 = jnp.float32
BF16 = jnp.bfloat16

D_MODEL = 1024
DEPTH = 4
PAGE_SIZE = 128
SSM_WIDTH = 512
SSM_GROUP_CH = 16
SSM_GROUPS = 32
SSM_STATE = 64
ATT_WIDTH = 512
HEAD_DIM = 64
ATT_HEADS = 8
ATT_SCALE = HEAD_DIM ** -0.5
IN_MAIN = SSM_WIDTH + 3 * ATT_WIDTH
N_EXPERTS = 32
TOP_K = 4
D_EXPERT = 1024
SWIGLU_LIMIT = 7.0
SWIGLU_ALPHA = 1.702
N_MOD = 6
NORM_EPS = 1e-6
LOG2E = math.log2(math.e)

LANES = 128
SSM_CHUNK = 8
SSM_BLOCKS = SSM_WIDTH // LANES
GROUPS_PER_BLOCK = LANES // SSM_GROUP_CH
STATES_PER_BLOCK = GROUPS_PER_BLOCK * SSM_STATE
VMEM_LIMIT = 56 * 1024 * 1024


def _cparams(sem):
    return pltpu.CompilerParams(dimension_semantics=sem, vmem_limit_bytes=VMEM_LIMIT)


def _split2(x):
    hi = x.astype(BF16)
    lo = (x - hi.astype(F32)).astype(BF16)
    return hi, lo


def _split3(x):
    hi = x.astype(BF16)
    r = x - hi.astype(F32)
    mid = r.astype(BF16)
    lo = (r - mid.astype(F32)).astype(BF16)
    return hi, mid, lo


def _dot(a, b):
    return jnp.dot(a, b, preferred_element_type=F32)


def _dot_x3(a, b):
    ah, al = _split2(a)
    bh, bl = _split2(b)
    return _dot(ah, bh) + (_dot(ah, bl) + _dot(al, bh))


def _dot_nt(a, b):
    return lax.dot_general(a, b, (((1,), (1,)), ((), ())), preferred_element_type=F32)


def _rms(x, g):
    return x * lax.rsqrt(jnp.mean(x * x, axis=-1, keepdims=True) + NORM_EPS) * g


def _adaln_kernel(c_ref, w_ref, b_ref, o_ref):
    c = c_ref[...]
    a = c * jax.nn.sigmoid(c)
    o_ref[...] = _dot_x3(a, w_ref[...]) + b_ref[...]


def _adaln(c_all, w_ada, b_ada):
    rows = c_all.shape[0]
    tn = 1536
    nt = (N_MOD * D_MODEL) // tn
    return pl.pallas_call(
        _adaln_kernel,
        grid=(DEPTH, nt),
        in_specs=[
            pl.BlockSpec((rows, D_MODEL), lambda l, n: (0, 0)),
            pl.BlockSpec((None, D_MODEL, tn), lambda l, n: (l, 0, n)),
            pl.BlockSpec((None, 1, tn), lambda l, n: (l, 0, n)),
        ],
        out_specs=pl.BlockSpec((None, rows, tn), lambda l, n: (l, 0, n)),
        out_shape=jax.ShapeDtypeStruct((DEPTH, rows, N_MOD * D_MODEL), F32),
        compiler_params=_cparams(("arbitrary", "arbitrary")),
        name="adaln",
    )(c_all, w_ada, b_ada.reshape(DEPTH, 1, N_MOD * D_MODEL))


def _log_sigmoid(x):
    return jnp.minimum(x, 0.0) - jnp.log1p(jnp.exp(-jnp.abs(x)))


def _inproj_kernel(x_ref, shift_ref, scale_ref, g1_ref, wm_ref, wf_ref, bf_ref, gq_ref, gk_ref,
                   seg_ref, tri_ref, place_ref, kone_ref,
                   u_ref, q_ref, k_ref, v_ref, kcat_ref, vcat_ref, lf_ref, fc_ref, carry_ref,
                   *, hi_prec, with_cumsum):
    x = x_ref[...]
    h = _rms(x, g1_ref[...]) * (1.0 + scale_ref[...]) + shift_ref[...]
    if hi_prec:
        z = _dot_x3(h, wm_ref[...])
        zf = _dot_x3(h, wf_ref[...])
    else:
        hb = h.astype(BF16)
        z = _dot(hb, wm_ref[...])
        zf = _dot(hb, wf_ref[...])
    u_ref[...] = z[:, :SSM_WIDTH]
    q = z[:, SSM_WIDTH:SSM_WIDTH + ATT_WIDTH]
    k = z[:, SSM_WIDTH + ATT_WIDTH:SSM_WIDTH + 2 * ATT_WIDTH]
    v = z[:, SSM_WIDTH + 2 * ATT_WIDTH:]
    seg = seg_ref[...]

    def head_norm(t, g):
        hi, lo = _split2(t * t)
        ms = _dot(hi, seg) + _dot(lo, seg)
        return t * lax.rsqrt(ms + NORM_EPS) * g

    qn = head_norm(q, gq_ref[...])
    kn = head_norm(k, gk_ref[...])
    q_ref[...] = (qn * (ATT_SCALE * LOG2E if with_cumsum else ATT_SCALE)).astype(BF16)
    k_ref[...] = kn
    v_ref[...] = v
    logf = _log_sigmoid(zf + bf_ref[...])
    lf_ref[...] = logf
    if with_cumsum:
        @pl.when(pl.program_id(1) == 0)
        def _():
            carry_ref[...] = jnp.zeros_like(carry_ref)

        tri = tri_ref[...]
        hi, mid, lo = _split3(logf)
        fc = carry_ref[...] + (_dot(tri, hi) + (_dot(tri, mid) + _dot(tri, lo)))
        carry_ref[...] = fc[-1:, :]
        fc = fc * LOG2E
        fc_ref[...] = fc
        hi, mid, lo = _split3(fc)
        kext = (_dot(hi, place_ref[0]) + (_dot(mid, place_ref[1]) + _dot(lo, place_ref[2]))
                + kone_ref[...]).astype(BF16)
        kb = kn.astype(BF16)
        vb = v.astype(BF16)
        lane = lax.broadcasted_iota(jnp.int32, (x.shape[0], LANES), 1)
        vext = jnp.where(lane == 0, 1.0, 0.0).astype(BF16)
        for p in range(ATT_HEADS // 2):
            lo_l, hi_l = p * LANES, (p + 1) * LANES
            kcat_ref[:, 2 * lo_l:2 * lo_l + LANES] = kb[:, lo_l:hi_l]
            kcat_ref[:, 2 * lo_l + LANES:2 * hi_l] = kext[:, lo_l:hi_l]
            vcat_ref[:, 2 * lo_l:2 * lo_l + LANES] = vb[:, lo_l:hi_l]
            vcat_ref[:, 2 * lo_l + LANES:2 * hi_l] = vext
    else:
        fc_ref[...] = logf
        kcat_ref[...] = jnp.zeros_like(kcat_ref)
        vcat_ref[...] = jnp.zeros_like(vcat_ref)


def _seg_matrix():
    i = np.arange(ATT_WIDTH) // HEAD_DIM
    return jnp.asarray((i[:, None] == i[None, :]).astype(np.float32) / HEAD_DIM, dtype=BF16)


def _tri_matrix(n, strict=False):
    i = np.arange(n)
    m = (i[None, :] < i[:, None]) if strict else (i[None, :] <= i[:, None])
    return jnp.asarray(m.astype(np.float32), dtype=BF16)


N_BIAS = 3


def _bias_placement():
    place = np.zeros((N_BIAS, LANES, ATT_WIDTH), np.float32)
    kone = np.zeros((1, ATT_WIDTH), np.float32)
    for h in range(ATT_HEADS):
        base = (h // 2) * LANES
        for part in range(N_BIAS):
            place[part, h, base + N_BIAS * (1 + h % 2) + part] = -1.0
    for p in range(ATT_HEADS // 2):
        kone[0, p * LANES:p * LANES + N_BIAS] = 1.0
    return jnp.asarray(place, dtype=BF16), jnp.asarray(kone)


def _in_proj(x, shift, scale, g1, w_main, w_f, b_f, g_q, g_k, *, tile, hi_prec, with_cumsum):
    s_dim, r_dim, _ = x.shape
    rm = shift.shape[1]
    nt = r_dim // tile
    row = lambda s, t: (s, t, 0)
    const2 = lambda s, t: (0, 0)
    mod_map = (lambda s, t: (s, 0, 0)) if rm == 1 else row
    mod_blk = (None, 1, D_MODEL) if rm == 1 else (None, tile, D_MODEL)
    outs = [
        jax.ShapeDtypeStruct((s_dim, r_dim, SSM_WIDTH), F32),
        jax.ShapeDtypeStruct((s_dim, r_dim, ATT_WIDTH), BF16),
        jax.ShapeDtypeStruct((s_dim, r_dim, ATT_WIDTH), F32),
        jax.ShapeDtypeStruct((s_dim, r_dim, ATT_WIDTH), F32),
        jax.ShapeDtypeStruct((s_dim, r_dim, 2 * ATT_WIDTH), BF16),
        jax.ShapeDtypeStruct((s_dim, r_dim, 2 * ATT_WIDTH), BF16),
        jax.ShapeDtypeStruct((s_dim, r_dim, LANES), F32),
        jax.ShapeDtypeStruct((s_dim, r_dim, LANES), F32),
    ]
    wide = pl.BlockSpec((None, tile, ATT_WIDTH), row)
    cat = pl.BlockSpec((None, tile, 2 * ATT_WIDTH), row)
    narrow = pl.BlockSpec((None, tile, LANES), row)
    place, kone = _bias_placement()
    return pl.pallas_call(
        functools.partial(_inproj_kernel, hi_prec=hi_prec, with_cumsum=with_cumsum),
        grid=(s_dim, nt),
        in_specs=[
            pl.BlockSpec((None, tile, D_MODEL), row),
            pl.BlockSpec(mod_blk, mod_map),
            pl.BlockSpec(mod_blk, mod_map),
            pl.BlockSpec((1, D_MODEL), const2),
            pl.BlockSpec(w_main.shape, const2),
            pl.BlockSpec(w_f.shape, const2),
            pl.BlockSpec((1, LANES), const2),
            pl.BlockSpec((1, ATT_WIDTH), const2),
            pl.BlockSpec((1, ATT_WIDTH), const2),
            pl.BlockSpec((ATT_WIDTH, ATT_WIDTH), const2),
            pl.BlockSpec((tile, tile), const2),
            pl.BlockSpec(place.shape, lambda s, t: (0, 0, 0)),
            pl.BlockSpec(kone.shape, const2),
        ],
        out_specs=[wide, wide, wide, wide, cat, cat, narrow, narrow],
        out_shape=outs,
        scratch_shapes=[pltpu.VMEM((1, LANES), F32)],
        compiler_params=_cparams(("arbitrary", "arbitrary")),
        name="in_proj_hi" if hi_prec else "in_proj",
    )(x, shift, scale, g1, w_main, w_f, b_f, g_q, g_k, _seg_matrix(), _tri_matrix(tile), place, kone)


def _discretize(lam_re, lam_im, log_dt):
    dt = jnp.exp(log_dt)
    mag = jnp.exp(lam_re * dt)
    ang = lam_im * dt
    a_re = mag * jnp.cos(ang)
    a_im = mag * jnp.sin(ang)
    den = lam_re * lam_re + lam_im * lam_im
    nr = a_re - 1.0
    coef_re = (nr * lam_re + a_im * lam_im) / den
    coef_im = (a_im * lam_re - nr * lam_im) / den
    return a_re, a_im, coef_re, coef_im


def _powers(a_re, a_im, n):
    pr, pi = [jnp.ones_like(a_re)], [jnp.zeros_like(a_im)]
    for _ in range(n):
        r, i = pr[-1], pi[-1]
        pr.append(r * a_re - i * a_im)
        pi.append(r * a_im + i * a_re)
    return pr, pi


def _ssm_prep_kernel(lr_row, li_row, dt_row, lr_col, li_col, dt_col, btr_ref, bti_ref, ctr_ref, cti_ref,
                     d_ref, tmat_ref, wz_ref, wc_ref, apow_ref, bbt_ref):
    L, W = SSM_CHUNK, LANES
    ar, ai, cr, ci = _discretize(lr_row[...], li_row[...], dt_row[...])
    acr, aci, _, _ = _discretize(lr_col[...], li_col[...], dt_col[...])
    btr, bti = btr_ref[...], bti_ref[...]
    bb_re = cr * btr - ci * bti
    bb_im = cr * bti + ci * btr
    bbt_ref[0] = bb_re
    bbt_ref[1] = bb_im
    ctr, cti = ctr_ref[...], cti_ref[...]
    pr, pi = _powers(ar, ai, L)
    pcr, pci = _powers(acr, aci, L)
    apow_ref[...] = jnp.concatenate(
        [ar, ai, pr[L], pi[L], jnp.zeros((4, STATES_PER_BLOCK), F32)], axis=0)
    eye = (lax.broadcasted_iota(jnp.int32, (W, W), 0) == lax.broadcasted_iota(jnp.int32, (W, W), 1))
    kts = []
    for tau in range(L):
        m_re = pr[tau] * bb_re - pi[tau] * bb_im
        m_im = pr[tau] * bb_im + pi[tau] * bb_re
        kt = _dot_x3(m_re, ctr) - _dot_x3(m_im, cti)
        if tau == 0:
            kt = kt + jnp.where(eye, d_ref[...], 0.0)
        kts.append(kt.astype(BF16))
        jp = L - 1 - tau
        wz_ref[jp * W:(jp + 1) * W, 0:STATES_PER_BLOCK] = m_re.astype(BF16)
        wz_ref[jp * W:(jp + 1) * W, STATES_PER_BLOCK:] = m_im.astype(BF16)
    zero = jnp.zeros((W, W), BF16)
    for jp in range(L):
        for j in range(L):
            tmat_ref[jp * W:(jp + 1) * W, j * W:(j + 1) * W] = kts[j - jp] if j >= jp else zero
    for j in range(L):
        qr, qi = pcr[j + 1], pci[j + 1]
        wc_ref[0:STATES_PER_BLOCK, j * W:(j + 1) * W] = (ctr * qr - cti * qi).astype(BF16)
        wc_ref[STATES_PER_BLOCK:, j * W:(j + 1) * W] = (-(ctr * qi + cti * qr)).astype(BF16)


def _blockdiag(t):
    a, b = t.shape[-2:]
    t = t.reshape(DEPTH, SSM_BLOCKS, GROUPS_PER_BLOCK, a, b)
    eye = jnp.eye(GROUPS_PER_BLOCK, dtype=t.dtype)
    out = jnp.einsum('lkgab,gh->lkgahb', t, eye)
    return out.reshape(DEPTH, SSM_BLOCKS, GROUPS_PER_BLOCK * a, GROUPS_PER_BLOCK * b)


def _ssm_prep(lam_re, lam_im, log_dt, b_re, b_im, c_re, c_im, d):
    S, KW = STATES_PER_BLOCK, SSM_CHUNK * LANES
    ldt = jnp.broadcast_to(log_dt[:, :, None], lam_re.shape)
    rows = [t.reshape(DEPTH, SSM_BLOCKS, 1, S) for t in (lam_re, lam_im, ldt)]
    cols = [t.reshape(DEPTH, SSM_BLOCKS, S, 1) for t in (lam_re, lam_im, ldt)]
    bt = [_blockdiag(jnp.swapaxes(t, -1, -2)) for t in (b_re, b_im)]
    ct = [_blockdiag(jnp.swapaxes(t, -1, -2)) for t in (c_re, c_im)]
    d4 = d.reshape(DEPTH, SSM_BLOCKS, 1, LANES)
    blk = lambda *s: pl.BlockSpec((None, None) + s, lambda l, k: (l, k) + (0,) * len(s))
    big = jax.ShapeDtypeStruct((DEPTH, SSM_BLOCKS, KW, KW), BF16)
    outs = pl.pallas_call(
        _ssm_prep_kernel,
        grid=(DEPTH, SSM_BLOCKS),
        in_specs=[blk(1, S)] * 3 + [blk(S, 1)] * 3 + [blk(LANES, S)] * 2 + [blk(S, LANES)] * 2
                 + [blk(1, LANES)],
        out_specs=[blk(KW, KW), blk(KW, KW), blk(KW, KW), blk(8, S), blk(2, LANES, S)],
        out_shape=[big, big, big,
                   jax.ShapeDtypeStruct((DEPTH, SSM_BLOCKS, 8, S), F32),
                   jax.ShapeDtypeStruct((DEPTH, SSM_BLOCKS, 2, LANES, S), F32)],
        compiler_params=_cparams(("arbitrary", "arbitrary")),
        name="ssm_prep",
    )(*rows, *cols, *bt, *ct, d4)
    tmat, wz, wc, apow, bbt = outs
    return dict(tmat=tmat, wz=wz, wc=wc, apow=apow, bbt=bbt, ct_re=ct[0], ct_im=ct[1], d4=d4)


SSM_SEQ_TILE = 2048
SSM_ROWS = SSM_SEQ_TILE // SSM_CHUNK


def _ssm_prompt_kernel(u_ref, tmat_ref, wz_ref, wc_ref, apow_ref, y_ref, st_ref,
                       up_scr, z_scr, h_scr, carry_scr):
    L, W, S, R = SSM_CHUNK, LANES, STATES_PER_BLOCK, SSM_ROWS

    @pl.when(pl.program_id(2) == 0)
    def _():
        carry_scr[...] = jnp.zeros_like(carry_scr)

    for j in range(L):
        up_scr[:, j * W:(j + 1) * W] = u_ref[pl.ds(j, R, stride=L), :].astype(BF16)
    up = up_scr[...]
    z_scr[...] = _dot(up, wz_ref[...])
    a8r = apow_ref[2:3, :]
    a8i = apow_ref[3:4, :]

    def step(m, carry):
        hr, hi = carry
        h_scr[pl.ds(m, 1), 0:S] = hr
        h_scr[pl.ds(m, 1), S:] = hi
        zr = z_scr[pl.ds(m, 1), 0:S]
        zi = z_scr[pl.ds(m, 1), S:]
        return a8r * hr - a8i * hi + zr, a8r * hi + a8i * hr + zi

    hr, hi = lax.fori_loop(0, R, step, (carry_scr[0:1, :], carry_scr[1:2, :]))
    carry_scr[0:1, :] = hr
    carry_scr[1:2, :] = hi
    st_ref[...] = jnp.concatenate([hr, hi, jnp.zeros((6, S), F32)], axis=0)
    y = _dot(up, tmat_ref[...]) + _dot(h_scr[...].astype(BF16), wc_ref[...])
    for j in range(L):
        y_ref[pl.ds(j, R, stride=L), :] = y[:, j * W:(j + 1) * W]


def _ssm_prompt(u, prep, l):
    bsz, seq, _ = u.shape
    KW, S = SSM_CHUNK * LANES, STATES_PER_BLOCK
    wspec = pl.BlockSpec((None, None, KW, KW), lambda k, b, s: (l, k, 0, 0))
    y, st = pl.pallas_call(
        _ssm_prompt_kernel,
        grid=(SSM_BLOCKS, bsz, seq // SSM_SEQ_TILE),
        in_specs=[
            pl.BlockSpec((None, SSM_SEQ_TILE, LANES), lambda k, b, s: (b, s, k)),
            wspec, wspec, wspec,
            pl.BlockSpec((None, None, 8, S), lambda k, b, s: (l, k, 0, 0)),
        ],
        out_specs=[
            pl.BlockSpec((None, SSM_SEQ_TILE, LANES), lambda k, b, s: (b, s, k)),
            pl.BlockSpec((None, None, 8, S), lambda k, b, s: (b, k, 0, 0)),
        ],
        out_shape=[jax.ShapeDtypeStruct((bsz, seq, SSM_WIDTH), F32),
                   jax.ShapeDtypeStruct((bsz, SSM_BLOCKS, 8, S), F32)],
        scratch_shapes=[pltpu.VMEM((SSM_ROWS, KW), BF16), pltpu.VMEM((SSM_ROWS, KW), F32),
                        pltpu.VMEM((SSM_ROWS, KW), F32), pltpu.VMEM((2, S), F32)],
        compiler_params=_cparams(("arbitrary", "arbitrary", "arbitrary")),
        name="ssm_prompt",
    )(u, prep['tmat'], prep['wz'], prep['wc'], prep['apow'])
    s_re = st[:, :, 0, :].reshape(bsz, SSM_GROUPS, SSM_STATE)
    s_im = st[:, :, 1, :].reshape(bsz, SSM_GROUPS, SSM_STATE)
    return y, s_re, s_im


def _ssm_sample_kernel(u_ref, sre_ref, sim_ref, apow_ref, bbt_ref, ctr_ref, cti_ref, d_ref,
                       y_ref, ore_ref, oim_ref):
    a_re = apow_ref[0:1, :]
    a_im = apow_ref[1:2, :]
    bb_re, bb_im = bbt_ref[0], bbt_ref[1]
    ctr, cti = ctr_ref[...], cti_ref[...]
    s_re, s_im = sre_ref[...], sim_ref[...]
    for j in range(u_ref.shape[0]):
        u = u_ref[j]
        n_re = a_re * s_re - a_im * s_im + _dot_x3(u, bb_re)
        n_im = a_re * s_im + a_im * s_re + _dot_x3(u, bb_im)
        s_re, s_im = n_re, n_im
        y_ref[j] = _dot_x3(s_re, ctr) - _dot_x3(s_im, cti) + d_ref[...] * u
    ore_ref[...] = s_re
    oim_ref[...] = s_im


def _ssm_sample(u, state_re, state_im, prep, l):
    t, db, _ = u.shape
    S = STATES_PER_BLOCK
    lk = lambda *s: pl.BlockSpec((None, None) + s, lambda k: (l, k) + (0,) * len(s))
    st_in = pl.BlockSpec((None, db, S), lambda k: (l, 0, k))
    st_out = pl.BlockSpec((db, S), lambda k: (0, k))
    useq = pl.BlockSpec((t, db, LANES), lambda k: (0, 0, k))
    return pl.pallas_call(
        _ssm_sample_kernel,
        grid=(SSM_BLOCKS,),
        in_specs=[useq, st_in, st_in, lk(8, S), lk(2, LANES, S), lk(S, LANES), lk(S, LANES), lk(1, LANES)],
        out_specs=[useq, st_out, st_out],
        out_shape=[jax.ShapeDtypeStruct((t, db, SSM_WIDTH), F32),
                   jax.ShapeDtypeStruct((db, SSM_GROUPS * SSM_STATE), F32),
                   jax.ShapeDtypeStruct((db, SSM_GROUPS * SSM_STATE), F32)],
        compiler_params=_cparams(("arbitrary",)),
        name="ssm_sample",
    )(u, state_re, state_im, prep['apow'], prep['bbt'], prep['ct_re'], prep['ct_im'], prep['d4'])


ATT_TILE = 512
ATT_Q_TILE = 1024
SHIFT_LIMIT = 30.0
NEG_BIG = -1e30


def _attn_prompt_kernel(qi_ref, kj_ref, q_ref, k_ref, v_ref, fq_ref, fk_ref, o_ref,
                        qm_scr, fq_scr, m_scr, l_scr, acc_scr):
    pair = pl.program_id(1)
    step = pl.program_id(2)
    i = qi_ref[step]
    j = kj_ref[step]
    tq = q_ref.shape[0]
    lane = lax.broadcasted_iota(jnp.int32, (tq, LANES), 1)

    @pl.when(j == 0)
    def _():
        q = q_ref[...]
        fq = fq_ref[...]
        for hh in range(2):
            in_head = (lane >= HEAD_DIM * hh) & (lane < HEAD_DIM * (hh + 1))
            qm_scr[hh] = jnp.where(in_head, q, jnp.zeros_like(q))
            fq_scr[hh] = jnp.sum(jnp.where(lane == 2 * pair + hh, fq, 0.0), axis=1, keepdims=True)
            m_scr[hh] = jnp.full((tq, 1), NEG_BIG, F32)
            l_scr[hh] = jnp.zeros((tq, 1), F32)
            acc_scr[hh] = jnp.zeros((tq, LANES), F32)

    k = k_ref[...]
    v = v_ref[...]
    row = lax.broadcasted_iota(jnp.int32, (tq, tq), 0)
    col = lax.broadcasted_iota(jnp.int32, (tq, tq), 1)
    visible = (col <= row) | (j < i)
    for hh in range(2):
        fk = fk_ref[pl.ds(2 * pair + hh, 1), :]
        s = _dot_nt(qm_scr[hh], k) + (fq_scr[hh] - fk)
        s = jnp.where(visible, s, NEG_BIG)
        m_prev = m_scr[hh]
        m_new = jnp.maximum(m_prev, jnp.max(s, axis=1, keepdims=True))
        alpha = jnp.exp2(m_prev - m_new)
        p = jnp.exp2(s - m_new)
        l_scr[hh] = alpha * l_scr[hh] + jnp.sum(p, axis=1, keepdims=True)
        acc_scr[hh] = alpha * acc_scr[hh] + _dot(p.astype(BF16), v)
        m_scr[hh] = m_new

    @pl.when(j == i)
    def _():
        o_ref[...] = jnp.where(lane < HEAD_DIM, acc_scr[0] / l_scr[0], acc_scr[1] / l_scr[1])


def _tile_schedule(n, ratio=1):
    qi = np.concatenate([np.full(ratio * (i + 1), i, np.int32) for i in range(n)])
    kj = np.concatenate([np.arange(ratio * (i + 1), dtype=np.int32) for i in range(n)])
    return jnp.asarray(qi), jnp.asarray(kj)


def _attn_bounded_kernel(qi_ref, kj_ref, q_ref, k_ref, v_ref, fq_ref, o_ref, qa_scr, acc_scr):
    pair = pl.program_id(1)
    step = pl.program_id(2)
    i = qi_ref[step]
    j = kj_ref[step]
    tq, tk = q_ref.shape[0], k_ref.shape[0]
    ratio = tq // tk
    lane = lax.broadcasted_iota(jnp.int32, (tq, LANES), 1)

    @pl.when(j == 0)
    def _():
        q = q_ref[...]
        fq = fq_ref[...]
        for hh in range(2):
            in_head = (lane >= HEAD_DIM * hh) & (lane < HEAD_DIM * (hh + 1))
            qa_scr[hh, :, 0:LANES] = jnp.where(in_head, q, jnp.zeros_like(q))
            c = jnp.sum(jnp.where(lane == 2 * pair + hh, fq, 0.0), axis=1, keepdims=True)
            c_hi, c_mid, c_lo = [t.astype(F32) for t in _split3(c)]
            ones_at = (lane >= N_BIAS * (1 + hh)) & (lane < N_BIAS * (2 + hh))
            ext = jnp.where(lane == 0, c_hi, jnp.where(lane == 1, c_mid, jnp.where(lane == 2, c_lo,
                            jnp.where(ones_at, 1.0, 0.0))))
            qa_scr[hh, :, LANES:] = ext.astype(BF16)
            acc_scr[hh] = jnp.zeros(acc_scr.shape[1:], F32)

    def tile(causal):
        k = k_ref[...]
        v = v_ref[...]
        for hh in range(2):
            s = _dot_nt(qa_scr[hh], k)
            if causal:
                row = lax.broadcasted_iota(jnp.int32, (tq, tk), 0)
                col = lax.broadcasted_iota(jnp.int32, (tq, tk), 1)
                s = jnp.where(col - row <= i * tq - j * tk, s, NEG_BIG)
            acc_scr[hh] += _dot(jnp.exp2(s).astype(BF16), v)

    @pl.when(j < ratio * i)
    def _():
        tile(False)

    @pl.when(j >= ratio * i)
    def _():
        tile(True)

    @pl.when(j == ratio * i + ratio - 1)
    def _():
        a0, a1 = acc_scr[0], acc_scr[1]
        o_ref[...] = jnp.where(lane < HEAD_DIM, a0[:, :LANES] / a0[:, LANES:LANES + 1],
                               a1[:, :LANES] / a1[:, LANES:LANES + 1])


def _attn_bounded(q, kcat, vcat, fq_shifted):
    bsz, seq, _ = q.shape
    tq, tk = ATT_Q_TILE, ATT_TILE
    qi, kj = _tile_schedule(seq // tq, tq // tk)
    qmap = lambda b, p, s, qi, kj: (b, qi[s], p)
    kmap = lambda b, p, s, qi, kj: (b, kj[s], p)
    grid_spec = pltpu.PrefetchScalarGridSpec(
        num_scalar_prefetch=2,
        grid=(bsz, ATT_HEADS // 2, qi.shape[0]),
        in_specs=[
            pl.BlockSpec((None, tq, LANES), qmap),
            pl.BlockSpec((None, tk, 2 * LANES), kmap),
            pl.BlockSpec((None, tk, 2 * LANES), kmap),
            pl.BlockSpec((None, tq, LANES), lambda b, p, s, qi, kj: (b, qi[s], 0)),
        ],
        out_specs=pl.BlockSpec((None, tq, LANES), qmap),
        scratch_shapes=[pltpu.VMEM((2, tq, 2 * LANES), BF16), pltpu.VMEM((2, tq, 2 * LANES), F32)],
    )
    return pl.pallas_call(
        _attn_bounded_kernel,
        grid_spec=grid_spec,
        out_shape=jax.ShapeDtypeStruct((bsz, seq, ATT_WIDTH), F32),
        compiler_params=_cparams(("arbitrary", "arbitrary", "arbitrary")),
        name="attn_bounded",
    )(qi, kj, q, kcat, vcat, fq_shifted)


def _attn_prompt(q, kcat, vcat, fcum, fcum_t):
    bsz, seq, _ = q.shape
    t = ATT_TILE
    qi, kj = _tile_schedule(seq // t)
    qmap = lambda b, p, s, qi, kj: (b, qi[s], p)
    kmap = lambda b, p, s, qi, kj: (b, kj[s], 2 * p)
    grid_spec = pltpu.PrefetchScalarGridSpec(
        num_scalar_prefetch=2,
        grid=(bsz, ATT_HEADS // 2, qi.shape[0]),
        in_specs=[
            pl.BlockSpec((None, t, LANES), qmap),
            pl.BlockSpec((None, t, LANES), kmap),
            pl.BlockSpec((None, t, LANES), kmap),
            pl.BlockSpec((None, t, LANES), lambda b, p, s, qi, kj: (b, qi[s], 0)),
            pl.BlockSpec((None, ATT_HEADS, t), lambda b, p, s, qi, kj: (b, 0, kj[s])),
        ],
        out_specs=pl.BlockSpec((None, t, LANES), qmap),
        scratch_shapes=[pltpu.VMEM((2, t, LANES), BF16), pltpu.VMEM((2, t, 1), F32),
                        pltpu.VMEM((2, t, 1), F32), pltpu.VMEM((2, t, 1), F32),
                        pltpu.VMEM((2, t, LANES), F32)],
    )
    return pl.pallas_call(
        _attn_prompt_kernel,
        grid_spec=grid_spec,
        out_shape=jax.ShapeDtypeStruct((bsz, seq, ATT_WIDTH), F32),
        compiler_params=_cparams(("arbitrary", "arbitrary", "arbitrary")),
        name="attn_prompt",
    )(qi, kj, q, kcat, vcat, fcum, fcum_t)


PAGES_PER_STEP = 8
FLAT = PAGE_SIZE * ATT_HEADS


FPAST_ROWS = 512
FPAST_POOL = 512


def _fpast_mats(n_pages):
    src = np.arange(FLAT)
    dst = np.arange(FLAT)
    same = (src[:, None] % ATT_HEADS) == (dst[None, :] // PAGE_SIZE)
    within = same & ((src[:, None] // ATT_HEADS) <= (dst[None, :] % PAGE_SIZE))
    r = np.arange(FPAST_ROWS)
    before = ((r[:, None] // n_pages) == (r[None, :] // n_pages)) & ((r[None, :] % n_pages) < (r[:, None] % n_pages))
    to_bf = lambda m: jnp.asarray(np.asarray(m, np.float32), dtype=BF16)
    return to_bf(within), to_bf(same), to_bf(before)


def _fpast_kernel(pt_ref, lf_ref, within_ref, same_ref, before_ref, f_ref, x_scr):
    c = pl.program_id(1)

    @pl.when(c == 0)
    def _():
        x_scr[...] = jnp.zeros_like(x_scr)

    def times(parts, m):
        return _dot(parts[0], m) + (_dot(parts[1], m) + _dot(parts[2], m))

    pool = lax.broadcasted_iota(jnp.int32, (FPAST_ROWS, FPAST_POOL), 1) + c * FPAST_POOL
    sel = jnp.where(pool == pt_ref[...], 1.0, 0.0).astype(BF16)
    hi, mid, lo = _split3(lf_ref[...])
    x_scr[...] += _dot(sel, hi) + (_dot(sel, mid) + _dot(sel, lo))

    @pl.when(c == pl.num_programs(1) - 1)
    def _():
        parts = _split3(x_scr[...])
        in_page = times(parts, within_ref[...])
        totals = _split3(times(parts, same_ref[...]))
        before = before_ref[...]
        f_ref[...] = in_page + (_dot(before, totals[0]) + (_dot(before, totals[1]) + _dot(before, totals[2])))


def _fpast(logf_view, page_table, l):
    db, n_pages = page_table.shape
    n_pool = logf_view.shape[1]
    rows = db * n_pages
    assert rows % FPAST_ROWS == 0 and FPAST_ROWS % n_pages == 0 and n_pool % FPAST_POOL == 0
    within, same, before = _fpast_mats(n_pages)
    const = lambda g, c: (0, 0)
    return pl.pallas_call(
        _fpast_kernel,
        grid=(rows // FPAST_ROWS, n_pool // FPAST_POOL),
        in_specs=[pl.BlockSpec((FPAST_ROWS, 1), lambda g, c: (g, 0)),
                  pl.BlockSpec((None, FPAST_POOL, FLAT), lambda g, c: (l, c, 0)),
                  pl.BlockSpec((FLAT, FLAT), const), pl.BlockSpec((FLAT, FLAT), const),
                  pl.BlockSpec((FPAST_ROWS, FPAST_ROWS), const)],
        out_specs=pl.BlockSpec((FPAST_ROWS, FLAT), lambda g, c: (g, 0)),
        out_shape=jax.ShapeDtypeStruct((rows, FLAT), F32),
        scratch_shapes=[pltpu.VMEM((FPAST_ROWS, FLAT), F32)],
        compiler_params=_cparams(("arbitrary", "arbitrary")),
        name="fpast",
    )(page_table.reshape(rows, 1), logf_view, within, same, before)


def _attn_sample_kernel(pt_ref, *refs):
    n = PAGES_PER_STEP
    k_refs, v_refs = refs[:n], refs[n:2 * n]
    (f_ref, tot_ref, qblk_ref, q_ref, kn_ref, vn_ref, lfn_ref, bmask_ref, nmask_ref, ncum_ref,
     o_ref, m_scr, l_scr, acc_scr) = refs[2 * n:]
    c = pl.program_id(1)
    nq = q_ref.shape[0]
    t_new = nq // ATT_HEADS

    @pl.when(c == 0)
    def _():
        m_scr[...] = jnp.full(m_scr.shape, NEG_BIG, F32)
        l_scr[...] = jnp.zeros_like(l_scr)
        acc_scr[...] = jnp.zeros_like(acc_scr)

    qblk = qblk_ref[...]
    scores = []
    for pg in range(n):
        kt = k_refs[pg][...].reshape(ATT_WIDTH, PAGE_SIZE).astype(BF16)
        f = f_ref[pg]
        scores.append(_dot(qblk, kt) - jnp.concatenate([f] * t_new, axis=0))
    m_blk = functools.reduce(jnp.maximum, scores)
    m_prev = m_scr[...]
    m_new = jnp.maximum(m_prev, jnp.max(m_blk, axis=1, keepdims=True))
    alpha = jnp.exp(m_prev - m_new)
    l_new = alpha * l_scr[...]
    acc = alpha * acc_scr[...]
    for pg in range(n):
        p = jnp.exp(scores[pg] - m_new)
        l_new = l_new + jnp.sum(p, axis=1, keepdims=True)
        vt = v_refs[pg][...].reshape(ATT_WIDTH, PAGE_SIZE).astype(BF16)
        acc = acc + _dot_nt(p.astype(BF16), vt)
    m_scr[...] = m_new
    l_scr[...] = l_new
    acc_scr[...] = acc

    @pl.when(c == pl.num_programs(1) - 1)
    def _():
        own = acc_scr[...] * bmask_ref[...]
        past = own[:, 0:HEAD_DIM]
        for h in range(1, ATT_HEADS):
            past = past + own[:, h * HEAD_DIM:(h + 1) * HEAD_DIM]
        hi, mid, lo = _split3(lfn_ref[...])
        ncum = ncum_ref[...]
        fnew = _dot(hi, ncum) + (_dot(mid, ncum) + _dot(lo, ncum)) + tot_ref[...]
        kn = kn_ref[...].astype(BF16)
        s = _dot_nt(q_ref[...], kn) - fnew + nmask_ref[...]
        m_prev = m_scr[...]
        m_new = jnp.maximum(m_prev, jnp.max(s, axis=1, keepdims=True))
        alpha = jnp.exp(m_prev - m_new)
        p = jnp.exp(s - m_new)
        l_fin = alpha * l_scr[...] + jnp.sum(p, axis=1, keepdims=True)
        o_ref[...] = (alpha * past + _dot(p.astype(BF16), vn_ref[...].astype(BF16))) / l_fin


def _attn_sample(q32, kn32, vn32, lfn, cache_kt, cache_vt, fpast, ftot, page_table, l):
    db, n_pages = page_table.shape
    n = PAGES_PER_STEP
    nq = q32.shape[1]
    r = np.arange(nq)
    col = np.arange(ATT_WIDTH)
    own = (col[None, :] // HEAD_DIM) == (r[:, None] % ATT_HEADS)
    qblk = jnp.where(jnp.asarray(own)[None], jnp.tile(q32, (1, 1, ATT_HEADS)), jnp.zeros((), q32.dtype))
    ok = ((r[None, :] % ATT_HEADS) == (r[:, None] % ATT_HEADS)) & ((r[None, :] // ATT_HEADS) <= (r[:, None] // ATT_HEADS))
    nmask = np.where(ok, 0.0, NEG_BIG).astype(np.float32)
    ncum = (((r[:, None] % ATT_HEADS) == (r[None, :] % ATT_HEADS)) & (r[:, None] <= r[None, :])).astype(np.float32)
    pmap = lambda i: (lambda b, c, pt: (l, pt[b * n_pages + c * n + i], 0, 0, 0))
    bmap = lambda b, c, pt: (b, 0, 0)
    const = lambda b, c, pt: (0, 0)
    page = lambda i: pl.BlockSpec((None, None, ATT_HEADS, HEAD_DIM, PAGE_SIZE), pmap(i))
    grid_spec = pltpu.PrefetchScalarGridSpec(
        num_scalar_prefetch=1,
        grid=(db, n_pages // n),
        in_specs=[page(i) for i in range(n)] + [page(i) for i in range(n)] + [
            pl.BlockSpec((None, n, 8, LANES), lambda b, c, pt: (b, c, 0, 0)),
            pl.BlockSpec((None, 1, nq), bmap),
            pl.BlockSpec((None, nq, ATT_WIDTH), bmap),
            pl.BlockSpec((None, nq, HEAD_DIM), bmap),
            pl.BlockSpec((None, nq, HEAD_DIM), bmap),
            pl.BlockSpec((None, nq, HEAD_DIM), bmap),
            pl.BlockSpec((None, 1, nq), bmap),
            pl.BlockSpec((nq, ATT_WIDTH), const),
            pl.BlockSpec((nq, nq), const),
            pl.BlockSpec((nq, nq), const),
        ],
        out_specs=pl.BlockSpec((None, nq, HEAD_DIM), bmap),
        scratch_shapes=[pltpu.VMEM((nq, 1), F32), pltpu.VMEM((nq, 1), F32), pltpu.VMEM((nq, ATT_WIDTH), F32)],
    )
    return pl.pallas_call(
        _attn_sample_kernel,
        grid_spec=grid_spec,
        out_shape=jax.ShapeDtypeStruct((db, nq, HEAD_DIM), F32),
        compiler_params=_cparams(("arbitrary", "arbitrary")),
        name="attn_sample",
    )(page_table.reshape(-1), *([cache_kt] * n), *([cache_vt] * n), fpast, ftot, qblk, q32, kn32, vn32, lfn,
      jnp.asarray(own.astype(np.float32)), jnp.asarray(nmask), jnp.asarray(ncum, dtype=BF16))


def _merge_kernel(ys_ref, ya_ref, x_ref, gate1_ref, shift2_ref, scale2_ref, gs_ref, ga_ref, g2_ref,
                  wglu_ref, bglu_ref, wos_ref, woa_ref, wr_ref, br_ref, cnt_in_ref, strict_ref,
                  x1_ref, h2_ref, eidx_ref, gate_ref, rank_ref, cnt_out_ref, cnt_scr, *, hi_prec):
    first = (pl.program_id(0) == 0) & (pl.program_id(1) == 0)

    @pl.when(first)
    def _():
        cnt_scr[...] = cnt_in_ref[...]

    mm = _dot_x3 if hi_prec else (lambda a, b: _dot(a.astype(BF16), b))
    y = jax.nn.gelu(ys_ref[...])
    y = y * jax.nn.sigmoid(mm(y, wglu_ref[...]) + bglu_ref[...])
    mix = mm(_rms(y, gs_ref[...]), wos_ref[...]) + mm(_rms(ya_ref[...], ga_ref[...]), woa_ref[...])
    x1 = x_ref[...] + gate1_ref[...] * mix
    x1_ref[...] = x1
    h2 = _rms(x1, g2_ref[...]) * (1.0 + scale2_ref[...]) + shift2_ref[...]
    h2_ref[...] = h2
    logits = _dot_x3(h2, wr_ref[...]) + br_ref[...]
    rows = logits.shape[0]
    lane = lax.broadcasted_iota(jnp.int32, (rows, LANES), 1)
    lane_f = lane.astype(F32)
    vals, idxs = [], []
    for _ in range(TOP_K):
        m = jnp.max(logits, axis=1, keepdims=True)
        idx = jnp.min(jnp.where(logits == m, lane_f, float(LANES)), axis=1, keepdims=True)
        vals.append(m)
        idxs.append(idx)
        logits = jnp.where(lane_f == idx, -jnp.inf, logits)
    exps = [jnp.exp(v - vals[0]) for v in vals]
    den = exps[0] + exps[1] + exps[2] + exps[3]
    cnt = cnt_scr[...]
    strict = strict_ref[...]
    e_out = jnp.zeros((rows, LANES), F32)
    g_out = jnp.zeros((rows, LANES), F32)
    r_out = jnp.zeros((rows, LANES), F32)
    for kk in range(TOP_K):
        onehot = lane_f == idxs[kk]
        oh = jnp.where(onehot, 1.0, 0.0)
        before = _dot(strict, oh.astype(BF16)) + cnt
        rank = jnp.sum(jnp.where(onehot, before, 0.0), axis=1, keepdims=True)
        cnt = cnt + jnp.sum(oh, axis=0, keepdims=True)
        e_out = jnp.where(lane == kk, idxs[kk], e_out)
        g_out = jnp.where(lane == kk, exps[kk] / den, g_out)
        r_out = jnp.where(lane == kk, rank, r_out)
    cnt_scr[...] = cnt
    cnt_out_ref[...] = cnt
    eidx_ref[...] = e_out.astype(jnp.int32)
    gate_ref[...] = g_out
    rank_ref[...] = r_out.astype(jnp.int32)


def _merge(ys, ya, x, gate1, shift2, scale2, g_out_ssm, g_out_att, g2, w_glu, b_glu, w_out_s, w_out_a,
           w_router, b_router, cnt_in, *, tile, hi_prec):
    s_dim, r_dim, _ = x.shape
    rm = gate1.shape[1]
    row = lambda s, t: (s, t, 0)
    const2 = lambda s, t: (0, 0)
    mod_map = (lambda s, t: (s, 0, 0)) if rm == 1 else row
    mod_blk = (None, 1, D_MODEL) if rm == 1 else (None, tile, D_MODEL)
    full = lambda a: pl.BlockSpec(a.shape, const2)
    half = pl.BlockSpec((None, tile, SSM_WIDTH), row)
    wide = pl.BlockSpec((None, tile, D_MODEL), row)
    narrow = pl.BlockSpec((None, tile, LANES), row)
    strict = _tri_matrix(tile, strict=True)
    args = (ys, ya, x, gate1, shift2, scale2, g_out_ssm, g_out_att, g2, w_glu, b_glu, w_out_s, w_out_a,
            w_router, b_router, cnt_in, strict)
    return pl.pallas_call(
        functools.partial(_merge_kernel, hi_prec=hi_prec),
        grid=(s_dim, r_dim // tile),
        in_specs=[half, half, wide, pl.BlockSpec(mod_blk, mod_map), pl.BlockSpec(mod_blk, mod_map),
                  pl.BlockSpec(mod_blk, mod_map)] + [full(a) for a in args[6:]],
        out_specs=[wide, wide, narrow, narrow, narrow, pl.BlockSpec((1, LANES), const2)],
        out_shape=[jax.ShapeDtypeStruct((s_dim, r_dim, D_MODEL), F32),
                   jax.ShapeDtypeStruct((s_dim, r_dim, D_MODEL), F32),
                   jax.ShapeDtypeStruct((s_dim, r_dim, LANES), jnp.int32),
                   jax.ShapeDtypeStruct((s_dim, r_dim, LANES), F32),
                   jax.ShapeDtypeStruct((s_dim, r_dim, LANES), jnp.int32),
                   jax.ShapeDtypeStruct((1, LANES), F32)],
        scratch_shapes=[pltpu.VMEM((1, LANES), F32)],
        compiler_params=_cparams(("arbitrary", "arbitrary")),
        name="merge_hi" if hi_prec else "merge",
    )(*args)


MOE_TILE = 256
ROW_TILE = 128
DMA_UNROLL = 8


def _dispatch_kernel(fill_ref, dest_ref, hp_ref, hs_ref, xpad_ref, zero_scr, sem):
    i = pl.program_id(0)
    last = pl.num_programs(0) - 1
    rows = hp_ref.shape[0]

    def scatter(h_ref):
        def row_copy(src_row, dst_row):
            return pltpu.make_async_copy(h_ref.at[pl.ds(src_row, 1)], xpad_ref.at[pl.ds(dst_row, 1)], sem)

        def issue(t, carry):
            for kk in range(TOP_K):
                row_copy(t, dest_ref[0, t * TOP_K + kk]).start()
            return carry

        lax.fori_loop(0, rows, issue, 0, unroll=DMA_UNROLL // TOP_K)

        def drain(a, carry):
            row_copy(0, 0).wait()
            return carry

        lax.fori_loop(0, rows * TOP_K, drain, 0, unroll=DMA_UNROLL)

    @pl.when(i < last)
    def _():
        scatter(hp_ref)

    @pl.when(i == last)
    def _():
        scatter(hs_ref)
        zero_scr[...] = jnp.zeros_like(zero_scr)

        def zero_row(dst_row):
            return pltpu.make_async_copy(zero_scr.at[pl.ds(0, 1)], xpad_ref.at[pl.ds(dst_row, 1)], sem)

        def per_expert(e, carry):
            start, count = fill_ref[2 * e], fill_ref[2 * e + 1]

            def fill(a, c2):
                zero_row(start + a).start()
                return c2

            lax.fori_loop(0, count, fill, 0)

            def fill_wait(a, c2):
                zero_row(0).wait()
                return c2

            lax.fori_loop(0, count, fill_wait, 0)
            return carry

        lax.fori_loop(0, N_EXPERTS, per_expert, 0)

        def zero_tile(t):
            return pltpu.make_async_copy(zero_scr, xpad_ref.at[pl.ds(t * MOE_TILE, MOE_TILE)], sem)

        n_used = fill_ref[2 * N_EXPERTS]
        n_tiles = xpad_ref.shape[0] // MOE_TILE

        def tail(t, carry):
            zero_tile(t).start()
            return carry

        lax.fori_loop(n_used, n_tiles, tail, 0)

        def tail_wait(t, carry):
            zero_tile(0).wait()
            return carry

        lax.fori_loop(n_used, n_tiles, tail_wait, 0)


def _dispatch(h2p, h2s, dest, fill, cap):
    ntp = h2p.shape[0] // ROW_TILE
    nt = ntp + 1
    grid_spec = pltpu.PrefetchScalarGridSpec(
        num_scalar_prefetch=1,
        grid=(nt,),
        in_specs=[pl.BlockSpec((None, 1, ROW_TILE * TOP_K), lambda i, f: (i, 0, 0), memory_space=pltpu.SMEM),
                  pl.BlockSpec((ROW_TILE, D_MODEL), lambda i, f: (jnp.minimum(i, ntp - 1), 0)),
                  pl.BlockSpec((ROW_TILE, D_MODEL), lambda i, f: (0, 0))],
        out_specs=pl.BlockSpec(memory_space=pl.ANY),
        scratch_shapes=[pltpu.VMEM((MOE_TILE, D_MODEL), F32), pltpu.SemaphoreType.DMA(())],
    )
    return pl.pallas_call(
        _dispatch_kernel,
        grid_spec=grid_spec,
        out_shape=jax.ShapeDtypeStruct((cap, D_MODEL), F32),
        compiler_params=_cparams(("arbitrary",)),
        name="dispatch",
    )(fill, dest.reshape(nt, 1, ROW_TILE * TOP_K), h2p, h2s)


def _experts_kernel(te_ref, nu_ref, x_ref, wgu_ref, bgu_ref, wd_ref, bd_ref, y_ref, wgu_scr, wd_scr):
    i = pl.program_id(0)
    prev = te_ref[jnp.maximum(i - 1, 0)]
    fresh = (i == 0) | (te_ref[i] != prev)

    @pl.when(fresh)
    def _():
        wgu_scr[...] = wgu_ref[...].astype(BF16)
        wd_scr[...] = wd_ref[...].astype(BF16)

    @pl.when(i < nu_ref[0])
    def _():
        gu = _dot(x_ref[...].astype(BF16), wgu_scr[...]) + bgu_ref[...]
        g = jnp.minimum(gu[:, :D_EXPERT], SWIGLU_LIMIT)
        up = jnp.clip(gu[:, D_EXPERT:], -SWIGLU_LIMIT, SWIGLU_LIMIT)
        act = (up + 1.0) * (g * jax.nn.sigmoid(SWIGLU_ALPHA * g))
        y_ref[...] = _dot(act.astype(BF16), wd_scr[...]) + bd_ref[...]

    @pl.when(i >= nu_ref[0])
    def _():
        y_ref[...] = jnp.zeros_like(y_ref)


def _experts(xpad, tile_expert, n_used, w_gate_up, b_gate_up, w_down, b_down, l):
    cap = xpad.shape[0]
    n_tiles = cap // MOE_TILE
    tmap = lambda i, te, nu: (jnp.minimum(i, nu[0] - 1), 0)
    grid_spec = pltpu.PrefetchScalarGridSpec(
        num_scalar_prefetch=2,
        grid=(n_tiles,),
        in_specs=[
            pl.BlockSpec((MOE_TILE, D_MODEL), tmap),
            pl.BlockSpec((None, None, D_MODEL, 2 * D_EXPERT), lambda i, te, nu: (l, te[i], 0, 0)),
            pl.BlockSpec((None, None, 1, 2 * D_EXPERT), lambda i, te, nu: (l, te[i], 0, 0)),
            pl.BlockSpec((None, None, D_EXPERT, D_MODEL), lambda i, te, nu: (l, te[i], 0, 0)),
            pl.BlockSpec((None, None, 1, D_MODEL), lambda i, te, nu: (l, te[i], 0, 0)),
        ],
        out_specs=pl.BlockSpec((MOE_TILE, D_MODEL), lambda i, te, nu: (i, 0)),
        scratch_shapes=[pltpu.VMEM((D_MODEL, 2 * D_EXPERT), BF16), pltpu.VMEM((D_EXPERT, D_MODEL), BF16)],
    )
    return pl.pallas_call(
        _experts_kernel,
        grid_spec=grid_spec,
        out_shape=jax.ShapeDtypeStruct((cap, D_MODEL), F32),
        compiler_params=_cparams(("arbitrary",)),
        name="experts",
    )(tile_expert, n_used, xpad, w_gate_up, b_gate_up.reshape(DEPTH, N_EXPERTS, 1, 2 * D_EXPERT),
      w_down, b_down.reshape(DEPTH, N_EXPERTS, 1, D_MODEL))


def _combine_kernel(dest_ref, x1_ref, gate2_ref, g_ref, ypad_ref, o_ref, buf, sem):
    rows = x1_ref.shape[0]

    def row_copy(src_row, slot):
        return pltpu.make_async_copy(ypad_ref.at[pl.ds(src_row, 1)], buf.at[pl.ds(slot, 1)], sem)

    def issue(t, carry):
        for kk in range(TOP_K):
            row_copy(dest_ref[0, t * TOP_K + kk], kk * rows + t).start()
        return carry

    lax.fori_loop(0, rows, issue, 0, unroll=DMA_UNROLL // TOP_K)

    def drain(a, carry):
        row_copy(0, 0).wait()
        return carry

    lax.fori_loop(0, rows * TOP_K, drain, 0, unroll=DMA_UNROLL)
    g = g_ref[...]
    moe = jnp.zeros((rows, D_MODEL), F32)
    for kk in range(TOP_K):
        moe = moe + g[:, kk:kk + 1] * buf[kk * rows:(kk + 1) * rows, :]
    o_ref[...] = x1_ref[...] + gate2_ref[...] * moe


def _combine(x1, gate2, gates, dest, ypad):
    s_dim, r_dim, _ = x1.shape
    rm = gate2.shape[1]
    nt = r_dim // ROW_TILE
    row = lambda s, t: (s, t, 0)
    mod_map = (lambda s, t: (s, 0, 0)) if rm == 1 else row
    mod_blk = (None, 1, D_MODEL) if rm == 1 else (None, ROW_TILE, D_MODEL)
    return pl.pallas_call(
        _combine_kernel,
        grid=(s_dim, nt),
        in_specs=[pl.BlockSpec((None, 1, ROW_TILE * TOP_K), lambda s, t: (s * nt + t, 0, 0),
                               memory_space=pltpu.SMEM),
                  pl.BlockSpec((None, ROW_TILE, D_MODEL), row), pl.BlockSpec(mod_blk, mod_map),
                  pl.BlockSpec((None, ROW_TILE, LANES), row), pl.BlockSpec(memory_space=pl.ANY)],
        out_specs=pl.BlockSpec((None, ROW_TILE, D_MODEL), row),
        out_shape=jax.ShapeDtypeStruct(x1.shape, F32),
        scratch_shapes=[pltpu.VMEM((TOP_K * ROW_TILE, D_MODEL), F32), pltpu.SemaphoreType.DMA(())],
        compiler_params=_cparams(("arbitrary", "arbitrary")),
        name="combine",
    )(dest.reshape(s_dim * nt, 1, ROW_TILE * TOP_K), x1, gate2, gates, ypad)


def _mods(mod_rows):
    return [m for m in jnp.split(mod_rows, N_MOD, axis=-1)]


def kernel(x_prompt, x_sample, c_prompt, c_sample, cache_k, cache_v, cache_logf, state_ssm_re, state_ssm_im,
           page_table, w_ada, b_ada, g_norm1, g_norm2, w_in, b_forget, g_q, g_k, ssm_lam_re, ssm_lam_im,
           ssm_log_dt, ssm_b_re, ssm_b_im, ssm_c_re, ssm_c_im, ssm_d, w_glu, b_glu, g_out_ssm, g_out_att, w_out,
           w_router, b_router, w_gate_up, b_gate_up, w_down, b_down):
    bsz, seq, _ = x_prompt.shape
    db, t_new, _ = x_sample.shape
    n_pool = cache_k.shape[1]
    n_seq = bsz + db
    c_rows = -(-n_seq // 8) * 8
    c_all = jnp.concatenate([c_prompt, c_sample, jnp.zeros((c_rows - n_seq, D_MODEL), F32)], axis=0)
    mod = _adaln(c_all, w_ada, b_ada)
    prep = _ssm_prep(ssm_lam_re, ssm_lam_im, ssm_log_dt, ssm_b_re, ssm_b_im, ssm_c_re, ssm_c_im, ssm_d)
    logf_view = cache_logf.reshape(DEPTH, n_pool, FLAT)
    cache_kt = jnp.transpose(cache_k, (0, 1, 3, 4, 2))
    cache_vt = jnp.transpose(cache_v, (0, 1, 3, 4, 2))
    n_pages = page_table.shape[1]
    st_re = state_ssm_re.reshape(DEPTH, db, SSM_GROUPS * SSM_STATE)
    st_im = state_ssm_im.reshape(DEPTH, db, SSM_GROUPS * SSM_STATE)
    n_tok_p, n_tok_s = bsz * seq, db * t_new
    assert n_tok_s == ROW_TILE and n_tok_p % ROW_TILE == 0
    n_tiles = -(-(n_tok_p + n_tok_s) * TOP_K // MOE_TILE) + N_EXPERTS
    cap = n_tiles * MOE_TILE
    pad_f = LANES - ATT_HEADS
    pad_e = LANES - N_EXPERTS

    xp = x_prompt
    xs = jnp.swapaxes(x_sample, 0, 1)
    outs = {name: [] for name in ('kp', 'vp', 'lfp', 'srp', 'sip', 'ks', 'vs', 'lfs', 'srs', 'sis')}
    for l in range(DEPTH):
        mp = [m[:, None, :] for m in _mods(mod[l, :bsz])]
        ms = [jnp.broadcast_to(m[None], (t_new, db, D_MODEL)) for m in _mods(mod[l, bsz:n_seq])]
        w_main = w_in[l, :, :IN_MAIN]
        w_f = jnp.pad(w_in[l, :, IN_MAIN:], ((0, 0), (0, pad_f)))
        b_f = jnp.pad(b_forget[l], (0, pad_f))[None]
        gq = jnp.tile(g_q[l], ATT_HEADS)[None]
        gk = jnp.tile(g_k[l], ATT_HEADS)[None]
        g1 = g_norm1[l][None]
        g2 = g_norm2[l][None]
        w_os, w_oa = w_out[l, :SSM_WIDTH], w_out[l, SSM_WIDTH:]
        w_r = jnp.pad(w_router[l], ((0, 0), (0, pad_e)))
        b_r = jnp.pad(b_router[l], (0, pad_e), constant_values=NEG_BIG)[None]
        small = (g_out_ssm[l][None], g_out_att[l][None], g2)

        u, q, k, v, kcat, vcat, lf, fc = _in_proj(xp, mp[0], mp[1], g1, w_main.astype(BF16), w_f.astype(BF16),
                                                  b_f, gq, gk, tile=512, hi_prec=False, with_cumsum=True)
        y_ssm, s_re, s_im = _ssm_prompt(u, prep, l)
        bound = float(HEAD_DIM) * ATT_SCALE * jnp.max(jnp.abs(g_q[l])) * jnp.max(jnp.abs(g_k[l]))
        y_att = lax.cond(
            bound <= SHIFT_LIMIT,
            lambda: _attn_bounded(q, kcat, vcat, fc - bound * LOG2E),
            lambda: _attn_prompt(q, kcat, vcat, fc, jnp.swapaxes(fc[..., :ATT_HEADS], 1, 2)))
        x1p, h2p, e_p, gate_p, rank_p, cnt_p = _merge(
            y_ssm, y_att, xp, mp[2], mp[3], mp[4], *small, w_glu[l].astype(BF16), b_glu[l][None],
            w_os.astype(BF16), w_oa.astype(BF16), w_r, b_r, jnp.zeros((1, LANES), F32), tile=512, hi_prec=False)
        outs['kp'].append(k.reshape(bsz, seq, ATT_HEADS, HEAD_DIM))
        outs['vp'].append(v.reshape(bsz, seq, ATT_HEADS, HEAD_DIM))
        outs['lfp'].append(lf[..., :ATT_HEADS])
        outs['srp'].append(s_re)
        outs['sip'].append(s_im)

        u_s, q_s, k_s, v_s, _, _, lf_s, _ = _in_proj(xs, ms[0], ms[1], g1, w_main, w_f, b_f, gq, gk,
                                                     tile=db, hi_prec=True, with_cumsum=False)
        y_ssm_s, o_re, o_im = _ssm_sample(u_s, st_re, st_im, prep, l)
        per_seq = lambda a: jnp.swapaxes(a, 0, 1).reshape(db, t_new * ATT_HEADS, HEAD_DIM)
        lfn = jnp.swapaxes(lf_s[..., :ATT_HEADS], 0, 1).reshape(db, 1, t_new * ATT_HEADS)
        fpast = _fpast(logf_view, page_table, l).reshape(db, n_pages, ATT_HEADS, PAGE_SIZE)
        ftot = jnp.tile(fpast[:, -1, :, -1], (1, t_new)).reshape(db, 1, t_new * ATT_HEADS)
        o_s = _attn_sample(per_seq(q_s), per_seq(k_s), per_seq(v_s), lfn, cache_kt, cache_vt, fpast, ftot,
                           page_table, l)
        y_att_s = jnp.swapaxes(o_s.reshape(db, t_new, ATT_WIDTH), 0, 1)
        x1s, h2s, e_s, gate_s, rank_s, cnt = _merge(
            y_ssm_s, y_att_s, xs, ms[2], ms[3], ms[4], *small, w_glu[l], b_glu[l][None], w_os, w_oa, w_r, b_r,
            cnt_p, tile=db, hi_prec=True)
        outs['ks'].append(jnp.swapaxes(k_s, 0, 1).reshape(db, t_new, ATT_HEADS, HEAD_DIM))
        outs['vs'].append(jnp.swapaxes(v_s, 0, 1).reshape(db, t_new, ATT_HEADS, HEAD_DIM))
        outs['lfs'].append(jnp.swapaxes(lf_s[..., :ATT_HEADS], 0, 1))
        outs['srs'].append(o_re.reshape(db, SSM_GROUPS, SSM_STATE))
        outs['sis'].append(o_im.reshape(db, SSM_GROUPS, SSM_STATE))

        counts = cnt[0, :N_EXPERTS].astype(jnp.int32)
        padded = (counts + MOE_TILE - 1) // MOE_TILE * MOE_TILE
        pend = jnp.cumsum(padded)
        pstart = pend - padded
        experts = jnp.arange(N_EXPERTS, dtype=jnp.int32)

        def slots(e, rank):
            hit = e[..., :TOP_K, None] == experts
            return (jnp.sum(jnp.where(hit, pstart, 0), axis=-1) + rank[..., :TOP_K]).reshape(-1)

        dest_p = slots(e_p, rank_p)
        dest_s = slots(e_s, rank_s)
        tile_start = jnp.arange(n_tiles, dtype=jnp.int32) * MOE_TILE
        tile_expert = jnp.sum((pend[None, :] <= tile_start[:, None]).astype(jnp.int32), axis=1)
        tile_expert = jnp.minimum(tile_expert, N_EXPERTS - 1)
        n_used = (pend[-1:] // MOE_TILE).astype(jnp.int32)
        fill = jnp.stack([pstart + counts, padded - counts], axis=-1).reshape(-1)
        fill = jnp.concatenate([fill, n_used]).astype(jnp.int32)
        xpad = _dispatch(h2p.reshape(n_tok_p, D_MODEL), h2s.reshape(n_tok_s, D_MODEL),
                         jnp.concatenate([dest_p, dest_s]), fill, cap)
        ypad = _experts(xpad, tile_expert, n_used, w_gate_up, b_gate_up, w_down, b_down, l)
        xp = _combine(x1p, mp[5], gate_p, dest_p, ypad)
        xs = _combine(x1s.reshape(1, n_tok_s, D_MODEL), ms[5].reshape(1, n_tok_s, D_MODEL),
                      gate_s.reshape(1, n_tok_s, LANES), dest_s, ypad).reshape(t_new, db, D_MODEL)

    st = lambda name: jnp.stack(outs[name])
    return (xp, jnp.swapaxes(xs, 0, 1), st('kp'), st('vp'), st('lfp'), st('srp'), st('sip'),
            st('ks'), st('vs'), st('lfs'), st('srs'), st('sis'))
```

```python
import functools
import math

import jax
import jax.numpy as jnp
import numpy as np
from jax import lax
from jax.experimental import pallas as pl
from jax.experimental.pallas import tpu as pltpu

F32 = jnp.float32
BF16 = jnp.bfloat16

D_MODEL = 1024
DEPTH = 4
PAGE_SIZE = 128
SSM_WIDTH = 512
SSM_GROUP_CH = 16
SSM_GROUPS = 32
SSM_STATE = 64
ATT_WIDTH = 512
HEAD_DIM = 64
ATT_HEADS = 8
ATT_SCALE = HEAD_DIM ** -0.5
IN_MAIN = SSM_WIDTH + 3 * ATT_WIDTH
N_EXPERTS = 32
TOP_K = 4
D_EXPERT = 1024
SWIGLU_LIMIT = 7.0
SWIGLU_ALPHA = 1.702
N_MOD = 6
NORM_EPS = 1e-6
LOG2E = math.log2(math.e)

LANES = 128
SSM_CHUNK = 8
SSM_BLOCKS = SSM_WIDTH // LANES
GROUPS_PER_BLOCK = LANES // SSM_GROUP_CH
STATES_PER_BLOCK = GROUPS_PER_BLOCK * SSM_STATE
VMEM_LIMIT = 56 * 1024 * 1024


def _cparams(sem):
    return pltpu.CompilerParams(dimension_semantics=sem, vmem_limit_bytes=VMEM_LIMIT)


def _split2(x):
    hi = x.astype(BF16)
    lo = (x - hi.astype(F32)).astype(BF16)
    return hi, lo


def _split3(x):
    hi = x.astype(BF16)
    r = x - hi.astype(F32)
    mid = r.astype(BF16)
    lo = (r - mid.astype(F32)).astype(BF16)
    return hi, mid, lo


def _dot(a, b):
    return jnp.dot(a, b, preferred_element_type=F32)


def _dot_x3(a, b):
    ah, al = _split2(a)
    bh, bl = _split2(b)
    return _dot(ah, bh) + (_dot(ah, bl) + _dot(al, bh))


def _dot_nt(a, b):
    return lax.dot_general(a, b, (((1,), (1,)), ((), ())), preferred_element_type=F32)


def _rms(x, g):
    return x * lax.rsqrt(jnp.mean(x * x, axis=-1, keepdims=True) + NORM_EPS) * g


def _adaln_kernel(c_ref, w_ref, b_ref, o_ref):
    c = c_ref[...]
    a = c * jax.nn.sigmoid(c)
    o_ref[...] = _dot_x3(a, w_ref[...]) + b_ref[...]


def _adaln(c_all, w_ada, b_ada):
    rows = c_all.shape[0]
    tn = 1536
    nt = (N_MOD * D_MODEL) // tn
    return pl.pallas_call(
        _adaln_kernel,
        grid=(DEPTH, nt),
        in_specs=[
            pl.BlockSpec((rows, D_MODEL), lambda l, n: (0, 0)),
            pl.BlockSpec((None, D_MODEL, tn), lambda l, n: (l, 0, n)),
            pl.BlockSpec((None, 1, tn), lambda l, n: (l, 0, n)),
        ],
        out_specs=pl.BlockSpec((None, rows, tn), lambda l, n: (l, 0, n)),
        out_shape=jax.ShapeDtypeStruct((DEPTH, rows, N_MOD * D_MODEL), F32),
        compiler_params=_cparams(("arbitrary", "arbitrary")),
        name="adaln",
    )(c_all, w_ada, b_ada.reshape(DEPTH, 1, N_MOD * D_MODEL))


def _log_sigmoid(x):
    return jnp.minimum(x, 0.0) - jnp.log1p(jnp.exp(-jnp.abs(x)))


def _inproj_kernel(x_ref, shift_ref, scale_ref, g1_ref, wm_ref, wf_ref, bf_ref, gq_ref, gk_ref,
                   seg_ref, tri_ref, place_ref, kone_ref,
                   u_ref, q_ref, k_ref, v_ref, kcat_ref, vcat_ref, lf_ref, fc_ref, carry_ref,
                   *, hi_prec, with_cumsum):
    x = x_ref[...]
    h = _rms(x, g1_ref[...]) * (1.0 + scale_ref[...]) + shift_ref[...]
    if hi_prec:
        z = _dot_x3(h, wm_ref[...])
        zf = _dot_x3(h, wf_ref[...])
    else:
        hb = h.astype(BF16)
        z = _dot(hb, wm_ref[...])
        zf = _dot(hb, wf_ref[...])
    u_ref[...] = z[:, :SSM_WIDTH]
    q = z[:, SSM_WIDTH:SSM_WIDTH + ATT_WIDTH]
    k = z[:, SSM_WIDTH + ATT_WIDTH:SSM_WIDTH + 2 * ATT_WIDTH]
    v = z[:, SSM_WIDTH + 2 * ATT_WIDTH:]
    seg = seg_ref[...]

    def head_norm(t, g):
        hi, lo = _split2(t * t)
        ms = _dot(hi, seg) + _dot(lo, seg)
        return t * lax.rsqrt(ms + NORM_EPS) * g

    qn = head_norm(q, gq_ref[...])
    kn = head_norm(k, gk_ref[...])
    q_ref[...] = (qn * (ATT_SCALE * LOG2E if with_cumsum else ATT_SCALE)).astype(BF16)
    k_ref[...] = kn
    v_ref[...] = v
    logf = _log_sigmoid(zf + bf_ref[...])
    lf_ref[...] = logf
    if with_cumsum:
        @pl.when(pl.program_id(1) == 0)
        def _():
            carry_ref[...] = jnp.zeros_like(carry_ref)

        tri = tri_ref[...]
        hi, mid, lo = _split3(logf)
        fc = carry_ref[...] + (_dot(tri, hi) + (_dot(tri, mid) + _dot(tri, lo)))
        carry_ref[...] = fc[-1:, :]
        fc = fc * LOG2E
        fc_ref[...] = fc
        hi, mid, lo = _split3(fc)
        kext = (_dot(hi, place_ref[0]) + (_dot(mid, place_ref[1]) + _dot(lo, place_ref[2]))
                + kone_ref[...]).astype(BF16)
        kb = kn.astype(BF16)
        vb = v.astype(BF16)
        lane = lax.broadcasted_iota(jnp.int32, (x.shape[0], LANES), 1)
        vext = jnp.where(lane == 0, 1.0, 0.0).astype(BF16)
        for p in range(ATT_HEADS // 2):
            lo_l, hi_l = p * LANES, (p + 1) * LANES
            kcat_ref[:, 2 * lo_l:2 * lo_l + LANES] = kb[:, lo_l:hi_l]
            kcat_ref[:, 2 * lo_l + LANES:2 * hi_l] = kext[:, lo_l:hi_l]
            vcat_ref[:, 2 * lo_l:2 * lo_l + LANES] = vb[:, lo_l:hi_l]
            vcat_ref[:, 2 * lo_l + LANES:2 * hi_l] = vext
    else:
        fc_ref[...] = logf
        kcat_ref[...] = jnp.zeros_like(kcat_ref)
        vcat_ref[...] = jnp.zeros_like(vcat_ref)


def _seg_matrix():
    i = np.arange(ATT_WIDTH) // HEAD_DIM
    return jnp.asarray((i[:, None] == i[None, :]).astype(np.float32) / HEAD_DIM, dtype=BF16)


def _tri_matrix(n, strict=False):
    i = np.arange(n)
    m = (i[None, :] < i[:, None]) if strict else (i[None, :] <= i[:, None])
    return jnp.asarray(m.astype(np.float32), dtype=BF16)


N_BIAS = 3


def _bias_placement():
    place = np.zeros((N_BIAS, LANES, ATT_WIDTH), np.float32)
    kone = np.zeros((1, ATT_WIDTH), np.float32)
    for h in range(ATT_HEADS):
        base = (h // 2) * LANES
        for part in range(N_BIAS):
            place[part, h, base + N_BIAS * (1 + h % 2) + part] = -1.0
    for p in range(ATT_HEADS // 2):
        kone[0, p * LANES:p * LANES + N_BIAS] = 1.0
    return jnp.asarray(place, dtype=BF16), jnp.asarray(kone)


def _in_proj(x, shift, scale, g1, w_main, w_f, b_f, g_q, g_k, *, tile, hi_prec, with_cumsum):
    s_dim, r_dim, _ = x.shape
    rm = shift.shape[1]
    nt = r_dim // tile
    row = lambda s, t: (s, t, 0)
    const2 = lambda s, t: (0, 0)
    mod_map = (lambda s, t: (s, 0, 0)) if rm == 1 else row
    mod_blk = (None, 1, D_MODEL) if rm == 1 else (None, tile, D_MODEL)
    outs = [
        jax.ShapeDtypeStruct((s_dim, r_dim, SSM_WIDTH), F32),
        jax.ShapeDtypeStruct((s_dim, r_dim, ATT_WIDTH), BF16),
        jax.ShapeDtypeStruct((s_dim, r_dim, ATT_WIDTH), F32),
        jax.ShapeDtypeStruct((s_dim, r_dim, ATT_WIDTH), F32),
        jax.ShapeDtypeStruct((s_dim, r_dim, 2 * ATT_WIDTH), BF16),
        jax.ShapeDtypeStruct((s_dim, r_dim, 2 * ATT_WIDTH), BF16),
        jax.ShapeDtypeStruct((s_dim, r_dim, LANES), F32),
        jax.ShapeDtypeStruct((s_dim, r_dim, LANES), F32),
    ]
    wide = pl.BlockSpec((None, tile, ATT_WIDTH), row)
    cat = pl.BlockSpec((None, tile, 2 * ATT_WIDTH), row)
    narrow = pl.BlockSpec((None, tile, LANES), row)
    place, kone = _bias_placement()
    return pl.pallas_call(
        functools.partial(_inproj_kernel, hi_prec=hi_prec, with_cumsum=with_cumsum),
        grid=(s_dim, nt),
        in_specs=[
            pl.BlockSpec((None, tile, D_MODEL), row),
            pl.BlockSpec(mod_blk, mod_map),
            pl.BlockSpec(mod_blk, mod_map),
            pl.BlockSpec((1, D_MODEL), const2),
            pl.BlockSpec(w_main.shape, const2),
            pl.BlockSpec(w_f.shape, const2),
            pl.BlockSpec((1, LANES), const2),
            pl.BlockSpec((1, ATT_WIDTH), const2),
            pl.BlockSpec((1, ATT_WIDTH), const2),
            pl.BlockSpec((ATT_WIDTH, ATT_WIDTH), const2),
            pl.BlockSpec((tile, tile), const2),
            pl.BlockSpec(place.shape, lambda s, t: (0, 0, 0)),
            pl.BlockSpec(kone.shape, const2),
        ],
        out_specs=[wide, wide, wide, wide, cat, cat, narrow, narrow],
        out_shape=outs,
        scratch_shapes=[pltpu.VMEM((1, LANES), F32)],
        compiler_params=_cparams(("arbitrary", "arbitrary")),
        name="in_proj_hi" if hi_prec else "in_proj",
    )(x, shift, scale, g1, w_main, w_f, b_f, g_q, g_k, _seg_matrix(), _tri_matrix(tile), place, kone)


def _discretize(lam_re, lam_im, log_dt):
    dt = jnp.exp(log_dt)
    mag = jnp.exp(lam_re * dt)
    ang = lam_im * dt
    a_re = mag * jnp.cos(ang)
    a_im = mag * jnp.sin(ang)
    den = lam_re * lam_re + lam_im * lam_im
    nr = a_re - 1.0
    coef_re = (nr * lam_re + a_im * lam_im) / den
    coef_im = (a_im * lam_re - nr * lam_im) / den
    return a_re, a_im, coef_re, coef_im


def _powers(a_re, a_im, n):
    pr, pi = [jnp.ones_like(a_re)], [jnp.zeros_like(a_im)]
    for _ in range(n):
        r, i = pr[-1], pi[-1]
        pr.append(r * a_re - i * a_im)
        pi.append(r * a_im + i * a_re)
    return pr, pi


def _ssm_prep_kernel(lr_row, li_row, dt_row, lr_col, li_col, dt_col, btr_ref, bti_ref, ctr_ref, cti_ref,
                     d_ref, tmat_ref, wz_ref, wc_ref, apow_ref, bbt_ref):
    L, W = SSM_CHUNK, LANES
    ar, ai, cr, ci = _discretize(lr_row[...], li_row[...], dt_row[...])
    acr, aci, _, _ = _discretize(lr_col[...], li_col[...], dt_col[...])
    btr, bti = btr_ref[...], bti_ref[...]
    bb_re = cr * btr - ci * bti
    bb_im = cr * bti + ci * btr
    bbt_ref[0] = bb_re
    bbt_ref[1] = bb_im
    ctr, cti = ctr_ref[...], cti_ref[...]
    pr, pi = _powers(ar, ai, L)
    pcr, pci = _powers(acr, aci, L)
    apow_ref[...] = jnp.concatenate(
        [ar, ai, pr[L], pi[L], jnp.zeros((4, STATES_PER_BLOCK), F32)], axis=0)
    eye = (lax.broadcasted_iota(jnp.int32, (W, W), 0) == lax.broadcasted_iota(jnp.int32, (W, W), 1))
    kts = []
    for tau in range(L):
        m_re = pr[tau] * bb_re - pi[tau] * bb_im
        m_im = pr[tau] * bb_im + pi[tau] * bb_re
        kt = _dot_x3(m_re, ctr) - _dot_x3(m_im, cti)
        if tau == 0:
            kt = kt + jnp.where(eye, d_ref[...], 0.0)
        kts.append(kt.astype(BF16))
        jp = L - 1 - tau
        wz_ref[jp * W:(jp + 1) * W, 0:STATES_PER_BLOCK] = m_re.astype(BF16)
        wz_ref[jp * W:(jp + 1) * W, STATES_PER_BLOCK:] = m_im.astype(BF16)
    zero = jnp.zeros((W, W), BF16)
    for jp in range(L):
        for j in range(L):
            tmat_ref[jp * W:(jp + 1) * W, j * W:(j + 1) * W] = kts[j - jp] if j >= jp else zero
    for j in range(L):
        qr, qi = pcr[j + 1], pci[j + 1]
        wc_ref[0:STATES_PER_BLOCK, j * W:(j + 1) * W] = (ctr * qr - cti * qi).astype(BF16)
        wc_ref[STATES_PER_BLOCK:, j * W:(j + 1) * W] = (-(ctr * qi + cti * qr)).astype(BF16)


def _blockdiag(t):
    a, b = t.shape[-2:]
    t = t.reshape(DEPTH, SSM_BLOCKS, GROUPS_PER_BLOCK, a, b)
    eye = jnp.eye(GROUPS_PER_BLOCK, dtype=t.dtype)
    out = jnp.einsum('lkgab,gh->lkgahb', t, eye)
    return out.reshape(DEPTH, SSM_BLOCKS, GROUPS_PER_BLOCK * a, GROUPS_PER_BLOCK * b)


def _ssm_prep(lam_re, lam_im, log_dt, b_re, b_im, c_re, c_im, d):
    S, KW = STATES_PER_BLOCK, SSM_CHUNK * LANES
    ldt = jnp.broadcast_to(log_dt[:, :, None], lam_re.shape)
    rows = [t.reshape(DEPTH, SSM_BLOCKS, 1, S) for t in (lam_re, lam_im, ldt)]
    cols = [t.reshape(DEPTH, SSM_BLOCKS, S, 1) for t in (lam_re, lam_im, ldt)]
    bt = [_blockdiag(jnp.swapaxes(t, -1, -2)) for t in (b_re, b_im)]
    ct = [_blockdiag(jnp.swapaxes(t, -1, -2)) for t in (c_re, c_im)]
    d4 = d.reshape(DEPTH, SSM_BLOCKS, 1, LANES)
    blk = lambda *s: pl.BlockSpec((None, None) + s, lambda l, k: (l, k) + (0,) * len(s))
    big = jax.ShapeDtypeStruct((DEPTH, SSM_BLOCKS, KW, KW), BF16)
    outs = pl.pallas_call(
        _ssm_prep_kernel,
        grid=(DEPTH, SSM_BLOCKS),
        in_specs=[blk(1, S)] * 3 + [blk(S, 1)] * 3 + [blk(LANES, S)] * 2 + [blk(S, LANES)] * 2
                 + [blk(1, LANES)],
        out_specs=[blk(KW, KW), blk(KW, KW), blk(KW, KW), blk(8, S), blk(2, LANES, S)],
        out_shape=[big, big, big,
                   jax.ShapeDtypeStruct((DEPTH, SSM_BLOCKS, 8, S), F32),
                   jax.ShapeDtypeStruct((DEPTH, SSM_BLOCKS, 2, LANES, S), F32)],
        compiler_params=_cparams(("arbitrary", "arbitrary")),
        name="ssm_prep",
    )(*rows, *cols, *bt, *ct, d4)
    tmat, wz, wc, apow, bbt = outs
    return dict(tmat=tmat, wz=wz, wc=wc, apow=apow, bbt=bbt, ct_re=ct[0], ct_im=ct[1], d4=d4)


SSM_SEQ_TILE = 2048
SSM_ROWS = SSM_SEQ_TILE // SSM_CHUNK


def _ssm_prompt_kernel(u_ref, tmat_ref, wz_ref, wc_ref, apow_ref, y_ref, st_ref,
                       up_scr, z_scr, h_scr, carry_scr):
    L, W, S, R = SSM_CHUNK, LANES, STATES_PER_BLOCK, SSM_ROWS

    @pl.when(pl.program_id(2) == 0)
    def _():
        carry_scr[...] = jnp.zeros_like(carry_scr)

    for j in range(L):
        up_scr[:, j * W:(j + 1) * W] = u_ref[pl.ds(j, R, stride=L), :].astype(BF16)
    up = up_scr[...]
    z_scr[...] = _dot(up, wz_ref[...])
    a8r = apow_ref[2:3, :]
    a8i = apow_ref[3:4, :]

    def step(m, carry):
        hr, hi = carry
        h_scr[pl.ds(m, 1), 0:S] = hr
        h_scr[pl.ds(m, 1), S:] = hi
        zr = z_scr[pl.ds(m, 1), 0:S]
        zi = z_scr[pl.ds(m, 1), S:]
        return a8r * hr - a8i * hi + zr, a8r * hi + a8i * hr + zi

    hr, hi = lax.fori_loop(0, R, step, (carry_scr[0:1, :], carry_scr[1:2, :]))
    carry_scr[0:1, :] = hr
    carry_scr[1:2, :] = hi
    st_ref[...] = jnp.concatenate([hr, hi, jnp.zeros((6, S), F32)], axis=0)
    y = _dot(up, tmat_ref[...]) + _dot(h_scr[...].astype(BF16), wc_ref[...])
    for j in range(L):
        y_ref[pl.ds(j, R, stride=L), :] = y[:, j * W:(j + 1) * W]


def _ssm_prompt(u, prep, l):
    bsz, seq, _ = u.shape
    KW, S = SSM_CHUNK * LANES, STATES_PER_BLOCK
    wspec = pl.BlockSpec((None, None, KW, KW), lambda k, b, s: (l, k, 0, 0))
    y, st = pl.pallas_call(
        _ssm_prompt_kernel,
        grid=(SSM_BLOCKS, bsz, seq // SSM_SEQ_TILE),
        in_specs=[
            pl.BlockSpec((None, SSM_SEQ_TILE, LANES), lambda k, b, s: (b, s, k)),
            wspec, wspec, wspec,
            pl.BlockSpec((None, None, 8, S), lambda k, b, s: (l, k, 0, 0)),
        ],
        out_specs=[
            pl.BlockSpec((None, SSM_SEQ_TILE, LANES), lambda k, b, s: (b, s, k)),
            pl.BlockSpec((None, None, 8, S), lambda k, b, s: (b, k, 0, 0)),
        ],
        out_shape=[jax.ShapeDtypeStruct((bsz, seq, SSM_WIDTH), F32),
                   jax.ShapeDtypeStruct((bsz, SSM_BLOCKS, 8, S), F32)],
        scratch_shapes=[pltpu.VMEM((SSM_ROWS, KW), BF16), pltpu.VMEM((SSM_ROWS, KW), F32),
                        pltpu.VMEM((SSM_ROWS, KW), F32), pltpu.VMEM((2, S), F32)],
        compiler_params=_cparams(("arbitrary", "arbitrary", "arbitrary")),
        name="ssm_prompt",
    )(u, prep['tmat'], prep['wz'], prep['wc'], prep['apow'])
    s_re = st[:, :, 0, :].reshape(bsz, SSM_GROUPS, SSM_STATE)
    s_im = st[:, :, 1, :].reshape(bsz, SSM_GROUPS, SSM_STATE)
    return y, s_re, s_im


def _ssm_sample_kernel(u_ref, sre_ref, sim_ref, apow_ref, bbt_ref, ctr_ref, cti_ref, d_ref,
                       y_ref, ore_ref, oim_ref):
    a_re = apow_ref[0:1, :]
    a_im = apow_ref[1:2, :]
    bb_re, bb_im = bbt_ref[0], bbt_ref[1]
    ctr, cti = ctr_ref[...], cti_ref[...]
    s_re, s_im = sre_ref[...], sim_ref[...]
    for j in range(u_ref.shape[0]):
        u = u_ref[j]
        n_re = a_re * s_re - a_im * s_im + _dot_x3(u, bb_re)
        n_im = a_re * s_im + a_im * s_re + _dot_x3(u, bb_im)
        s_re, s_im = n_re, n_im
        y_ref[j] = _dot_x3(s_re, ctr) - _dot_x3(s_im, cti) + d_ref[...] * u
    ore_ref[...] = s_re
    oim_ref[...] = s_im


def _ssm_sample(u, state_re, state_im, prep, l):
    t, db, _ = u.shape
    S = STATES_PER_BLOCK
    lk = lambda *s: pl.BlockSpec((None, None) + s, lambda k: (l, k) + (0,) * len(s))
    st_in = pl.BlockSpec((None, db, S), lambda k: (l, 0, k))
    st_out = pl.BlockSpec((db, S), lambda k: (0, k))
    useq = pl.BlockSpec((t, db, LANES), lambda k: (0, 0, k))
    return pl.pallas_call(
        _ssm_sample_kernel,
        grid=(SSM_BLOCKS,),
        in_specs=[useq, st_in, st_in, lk(8, S), lk(2, LANES, S), lk(S, LANES), lk(S, LANES), lk(1, LANES)],
        out_specs=[useq, st_out, st_out],
        out_shape=[jax.ShapeDtypeStruct((t, db, SSM_WIDTH), F32),
                   jax.ShapeDtypeStruct((db, SSM_GROUPS * SSM_STATE), F32),
                   jax.ShapeDtypeStruct((db, SSM_GROUPS * SSM_STATE), F32)],
        compiler_params=_cparams(("arbitrary",)),
        name="ssm_sample",
    )(u, state_re, state_im, prep['apow'], prep['bbt'], prep['ct_re'], prep['ct_im'], prep['d4'])


ATT_TILE = 512
ATT_Q_TILE = 1024
SHIFT_LIMIT = 30.0
NEG_BIG = -1e30


def _attn_prompt_kernel(qi_ref, kj_ref, q_ref, k_ref, v_ref, fq_ref, fk_ref, o_ref,
                        qm_scr, fq_scr, m_scr, l_scr, acc_scr):
    pair = pl.program_id(1)
    step = pl.program_id(2)
    i = qi_ref[step]
    j = kj_ref[step]
    tq = q_ref.shape[0]
    lane = lax.broadcasted_iota(jnp.int32, (tq, LANES), 1)

    @pl.when(j == 0)
    def _():
        q = q_ref[...]
        fq = fq_ref[...]
        for hh in range(2):
            in_head = (lane >= HEAD_DIM * hh) & (lane < HEAD_DIM * (hh + 1))
            qm_scr[hh] = jnp.where(in_head, q, jnp.zeros_like(q))
            fq_scr[hh] = jnp.sum(jnp.where(lane == 2 * pair + hh, fq, 0.0), axis=1, keepdims=True)
            m_scr[hh] = jnp.full((tq, 1), NEG_BIG, F32)
            l_scr[hh] = jnp.zeros((tq, 1), F32)
            acc_scr[hh] = jnp.zeros((tq, LANES), F32)

    k = k_ref[...]
    v = v_ref[...]
    row = lax.broadcasted_iota(jnp.int32, (tq, tq), 0)
    col = lax.broadcasted_iota(jnp.int32, (tq, tq), 1)
    visible = (col <= row) | (j < i)
    for hh in range(2):
        fk = fk_ref[pl.ds(2 * pair + hh, 1), :]
        s = _dot_nt(qm_scr[hh], k) + (fq_scr[hh] - fk)
        s = jnp.where(visible, s, NEG_BIG)
        m_prev = m_scr[hh]
        m_new = jnp.maximum(m_prev, jnp.max(s, axis=1, keepdims=True))
        alpha = jnp.exp2(m_prev - m_new)
        p = jnp.exp2(s - m_new)
        l_scr[hh] = alpha * l_scr[hh] + jnp.sum(p, axis=1, keepdims=True)
        acc_scr[hh] = alpha * acc_scr[hh] + _dot(p.astype(BF16), v)
        m_scr[hh] = m_new

    @pl.when(j == i)
    def _():
        o_ref[...] = jnp.where(lane < HEAD_DIM, acc_scr[0] / l_scr[0], acc_scr[1] / l_scr[1])


def _tile_schedule(n, ratio=1):
    qi = np.concatenate([np.full(ratio * (i + 1), i, np.int32) for i in range(n)])
    kj = np.concatenate([np.arange(ratio * (i + 1), dtype=np.int32) for i in range(n)])
    return jnp.asarray(qi), jnp.asarray(kj)


def _attn_bounded_kernel(qi_ref, kj_ref, q_ref, k_ref, v_ref, fq_ref, o_ref, qa_scr, acc_scr):
    pair = pl.program_id(1)
    step = pl.program_id(2)
    i = qi_ref[step]
    j = kj_ref[step]
    tq, tk = q_ref.shape[0], k_ref.shape[0]
    ratio = tq // tk
    lane = lax.broadcasted_iota(jnp.int32, (tq, LANES), 1)

    @pl.when(j == 0)
    def _():
        q = q_ref[...]
        fq = fq_ref[...]
        for hh in range(2):
            in_head = (lane >= HEAD_DIM * hh) & (lane < HEAD_DIM * (hh + 1))
            qa_scr[hh, :, 0:LANES] = jnp.where(in_head, q, jnp.zeros_like(q))
            c = jnp.sum(jnp.where(lane == 2 * pair + hh, fq, 0.0), axis=1, keepdims=True)
            c_hi, c_mid, c_lo = [t.astype(F32) for t in _split3(c)]
            ones_at = (lane >= N_BIAS * (1 + hh)) & (lane < N_BIAS * (2 + hh))
            ext = jnp.where(lane == 0, c_hi, jnp.where(lane == 1, c_mid, jnp.where(lane == 2, c_lo,
                            jnp.where(ones_at, 1.0, 0.0))))
            qa_scr[hh, :, LANES:] = ext.astype(BF16)
            acc_scr[hh] = jnp.zeros(acc_scr.shape[1:], F32)

    def tile(causal):
        k = k_ref[...]
        v = v_ref[...]
        for hh in range(2):
            s = _dot_nt(qa_scr[hh], k)
            if causal:
                row = lax.broadcasted_iota(jnp.int32, (tq, tk), 0)
                col = lax.broadcasted_iota(jnp.int32, (tq, tk), 1)
                s = jnp.where(col - row <= i * tq - j * tk, s, NEG_BIG)
            acc_scr[hh] += _dot(jnp.exp2(s).astype(BF16), v)

    @pl.when(j < ratio * i)
    def _():
        tile(False)

    @pl.when(j >= ratio * i)
    def _():
        tile(True)

    @pl.when(j == ratio * i + ratio - 1)
    def _():
        a0, a1 = acc_scr[0], acc_scr[1]
        o_ref[...] = jnp.where(lane < HEAD_DIM, a0[:, :LANES] / a0[:, LANES:LANES + 1],
                               a1[:, :LANES] / a1[:, LANES:LANES + 1])


def _attn_bounded(q, kcat, vcat, fq_shifted):
    bsz, seq, _ = q.shape
    tq, tk = ATT_Q_TILE, ATT_TILE
    qi, kj = _tile_schedule(seq // tq, tq // tk)
    qmap = lambda b, p, s, qi, kj: (b, qi[s], p)
    kmap = lambda b, p, s, qi, kj: (b, kj[s], p)
    grid_spec = pltpu.PrefetchScalarGridSpec(
        num_scalar_prefetch=2,
        grid=(bsz, ATT_HEADS // 2, qi.shape[0]),
        in_specs=[
            pl.BlockSpec((None, tq, LANES), qmap),
            pl.BlockSpec((None, tk, 2 * LANES), kmap),
            pl.BlockSpec((None, tk, 2 * LANES), kmap),
            pl.BlockSpec((None, tq, LANES), lambda b, p, s, qi, kj: (b, qi[s], 0)),
        ],
        out_specs=pl.BlockSpec((None, tq, LANES), qmap),
        scratch_shapes=[pltpu.VMEM((2, tq, 2 * LANES), BF16), pltpu.VMEM((2, tq, 2 * LANES), F32)],
    )
    return pl.pallas_call(
        _attn_bounded_kernel,
        grid_spec=grid_spec,
        out_shape=jax.ShapeDtypeStruct((bsz, seq, ATT_WIDTH), F32),
        compiler_params=_cparams(("arbitrary", "arbitrary", "arbitrary")),
        name="attn_bounded",
    )(qi, kj, q, kcat, vcat, fq_shifted)


def _attn_prompt(q, kcat, vcat, fcum, fcum_t):
    bsz, seq, _ = q.shape
    t = ATT_TILE
    qi, kj = _tile_schedule(seq // t)
    qmap = lambda b, p, s, qi, kj: (b, qi[s], p)
    kmap = lambda b, p, s, qi, kj: (b, kj[s], 2 * p)
    grid_spec = pltpu.PrefetchScalarGridSpec(
        num_scalar_prefetch=2,
        grid=(bsz, ATT_HEADS // 2, qi.shape[0]),
        in_specs=[
            pl.BlockSpec((None, t, LANES), qmap),
            pl.BlockSpec((None, t, LANES), kmap),
            pl.BlockSpec((None, t, LANES), kmap),
            pl.BlockSpec((None, t, LANES), lambda b, p, s, qi, kj: (b, qi[s], 0)),
            pl.BlockSpec((None, ATT_HEADS, t), lambda b, p, s, qi, kj: (b, 0, kj[s])),
        ],
        out_specs=pl.BlockSpec((None, t, LANES), qmap),
        scratch_shapes=[pltpu.VMEM((2, t, LANES), BF16), pltpu.VMEM((2, t, 1), F32),
                        pltpu.VMEM((2, t, 1), F32), pltpu.VMEM((2, t, 1), F32),
                        pltpu.VMEM((2, t, LANES), F32)],
    )
    return pl.pallas_call(
        _attn_prompt_kernel,
        grid_spec=grid_spec,
        out_shape=jax.ShapeDtypeStruct((bsz, seq, ATT_WIDTH), F32),
        compiler_params=_cparams(("arbitrary", "arbitrary", "arbitrary")),
        name="attn_prompt",
    )(qi, kj, q, kcat, vcat, fcum, fcum_t)


PAGES_PER_STEP = 8
FLAT = PAGE_SIZE * ATT_HEADS


FPAST_ROWS = 512
FPAST_POOL = 512


def _fpast_mats(n_pages):
    src = np.arange(FLAT)
    dst = np.arange(FLAT)
    same = (src[:, None] % ATT_HEADS) == (dst[None, :] // PAGE_SIZE)
    within = same & ((src[:, None] // ATT_HEADS) <= (dst[None, :] % PAGE_SIZE))
    r = np.arange(FPAST_ROWS)
    before = ((r[:, None] // n_pages) == (r[None, :] // n_pages)) & ((r[None, :] % n_pages) < (r[:, None] % n_pages))
    to_bf = lambda m: jnp.asarray(np.asarray(m, np.float32), dtype=BF16)
    return to_bf(within), to_bf(same), to_bf(before)


def _fpast_kernel(pt_ref, lf_ref, within_ref, same_ref, before_ref, f_ref, x_scr):
    c = pl.program_id(1)

    @pl.when(c == 0)
    def _():
        x_scr[...] = jnp.zeros_like(x_scr)

    def times(parts, m):
        return _dot(parts[0], m) + (_dot(parts[1], m) + _dot(parts[2], m))

    pool = lax.broadcasted_iota(jnp.int32, (FPAST_ROWS, FPAST_POOL), 1) + c * FPAST_POOL
    sel = jnp.where(pool == pt_ref[...], 1.0, 0.0).astype(BF16)
    hi, mid, lo = _split3(lf_ref[...])
    x_scr[...] += _dot(sel, hi) + (_dot(sel, mid) + _dot(sel, lo))

    @pl.when(c == pl.num_programs(1) - 1)
    def _():
        parts = _split3(x_scr[...])
        in_page = times(parts, within_ref[...])
        totals = _split3(times(parts, same_ref[...]))
        before = before_ref[...]
        f_ref[...] = in_page + (_dot(before, totals[0]) + (_dot(before, totals[1]) + _dot(before, totals[2])))


def _fpast(logf_view, page_table, l):
    db, n_pages = page_table.shape
    n_pool = logf_view.shape[1]
    rows = db * n_pages
    assert rows % FPAST_ROWS == 0 and FPAST_ROWS % n_pages == 0 and n_pool % FPAST_POOL == 0
    within, same, before = _fpast_mats(n_pages)
    const = lambda g, c: (0, 0)
    return pl.pallas_call(
        _fpast_kernel,
        grid=(rows // FPAST_ROWS, n_pool // FPAST_POOL),
        in_specs=[pl.BlockSpec((FPAST_ROWS, 1), lambda g, c: (g, 0)),
                  pl.BlockSpec((None, FPAST_POOL, FLAT), lambda g, c: (l, c, 0)),
                  pl.BlockSpec((FLAT, FLAT), const), pl.BlockSpec((FLAT, FLAT), const),
                  pl.BlockSpec((FPAST_ROWS, FPAST_ROWS), const)],
        out_specs=pl.BlockSpec((FPAST_ROWS, FLAT), lambda g, c: (g, 0)),
        out_shape=jax.ShapeDtypeStruct((rows, FLAT), F32),
        scratch_shapes=[pltpu.VMEM((FPAST_ROWS, FLAT), F32)],
        compiler_params=_cparams(("arbitrary", "arbitrary")),
        name="fpast",
    )(page_table.reshape(rows, 1), logf_view, within, same, before)


def _attn_sample_kernel(pt_ref, *refs):
    n = PAGES_PER_STEP
    k_refs, v_refs = refs[:n], refs[n:2 * n]
    (f_ref, tot_ref, qblk_ref, q_ref, kn_ref, vn_ref, lfn_ref, bmask_ref, nmask_ref, ncum_ref,
     o_ref, m_scr, l_scr, acc_scr) = refs[2 * n:]
    c = pl.program_id(1)
    nq = q_ref.shape[0]
    t_new = nq // ATT_HEADS

    @pl.when(c == 0)
    def _():
        m_scr[...] = jnp.full(m_scr.shape, NEG_BIG, F32)
        l_scr[...] = jnp.zeros_like(l_scr)
        acc_scr[...] = jnp.zeros_like(acc_scr)

    qblk = qblk_ref[...]
    scores = []
    for pg in range(n):
        kt = k_refs[pg][...].reshape(ATT_WIDTH, PAGE_SIZE).astype(BF16)
        f = f_ref[pg]
        scores.append(_dot(qblk, kt) - jnp.concatenate([f] * t_new, axis=0))
    m_blk = functools.reduce(jnp.maximum, scores)
    m_prev = m_scr[...]
    m_new = jnp.maximum(m_prev, jnp.max(m_blk, axis=1, keepdims=True))
    alpha = jnp.exp(m_prev - m_new)
    l_new = alpha * l_scr[...]
    acc = alpha * acc_scr[...]
    for pg in range(n):
        p = jnp.exp(scores[pg] - m_new)
        l_new = l_new + jnp.sum(p, axis=1, keepdims=True)
        vt = v_refs[pg][...].reshape(ATT_WIDTH, PAGE_SIZE).astype(BF16)
        acc = acc + _dot_nt(p.astype(BF16), vt)
    m_scr[...] = m_new
    l_scr[...] = l_new
    acc_scr[...] = acc

    @pl.when(c == pl.num_programs(1) - 1)
    def _():
        own = acc_scr[...] * bmask_ref[...]
        past = own[:, 0:HEAD_DIM]
        for h in range(1, ATT_HEADS):
            past = past + own[:, h * HEAD_DIM:(h + 1) * HEAD_DIM]
        hi, mid, lo = _split3(lfn_ref[...])
        ncum = ncum_ref[...]
        fnew = _dot(hi, ncum) + (_dot(mid, ncum) + _dot(lo, ncum)) + tot_ref[...]
        kn = kn_ref[...].astype(BF16)
        s = _dot_nt(q_ref[...], kn) - fnew + nmask_ref[...]
        m_prev = m_scr[...]
        m_new = jnp.maximum(m_prev, jnp.max(s, axis=1, keepdims=True))
        alpha = jnp.exp(m_prev - m_new)
        p = jnp.exp(s - m_new)
        l_fin = alpha * l_scr[...] + jnp.sum(p, axis=1, keepdims=True)
        o_ref[...] = (alpha * past + _dot(p.astype(BF16), vn_ref[...].astype(BF16))) / l_fin


def _attn_sample(q32, kn32, vn32, lfn, cache_kt, cache_vt, fpast, ftot, page_table, l):
    db, n_pages = page_table.shape
    n = PAGES_PER_STEP
    nq = q32.shape[1]
    r = np.arange(nq)
    col = np.arange(ATT_WIDTH)
    own = (col[None, :] // HEAD_DIM) == (r[:, None] % ATT_HEADS)
    qblk = jnp.where(jnp.asarray(own)[None], jnp.tile(q32, (1, 1, ATT_HEADS)), jnp.zeros((), q32.dtype))
    ok = ((r[None, :] % ATT_HEADS) == (r[:, None] % ATT_HEADS)) & ((r[None, :] // ATT_HEADS) <= (r[:, None] // ATT_HEADS))
    nmask = np.where(ok, 0.0, NEG_BIG).astype(np.float32)
    ncum = (((r[:, None] % ATT_HEADS) == (r[None, :] % ATT_HEADS)) & (r[:, None] <= r[None, :])).astype(np.float32)
    pmap = lambda i: (lambda b, c, pt: (l, pt[b * n_pages + c * n + i], 0, 0, 0))
    bmap = lambda b, c, pt: (b, 0, 0)
    const = lambda b, c, pt: (0, 0)
    page = lambda i: pl.BlockSpec((None, None, ATT_HEADS, HEAD_DIM, PAGE_SIZE), pmap(i))
    grid_spec = pltpu.PrefetchScalarGridSpec(
        num_scalar_prefetch=1,
        grid=(db, n_pages // n),
        in_specs=[page(i) for i in range(n)] + [page(i) for i in range(n)] + [
            pl.BlockSpec((None, n, 8, LANES), lambda b, c, pt: (b, c, 0, 0)),
            pl.BlockSpec((None, 1, nq), bmap),
            pl.BlockSpec((None, nq, ATT_WIDTH), bmap),
            pl.BlockSpec((None, nq, HEAD_DIM), bmap),
            pl.BlockSpec((None, nq, HEAD_DIM), bmap),
            pl.BlockSpec((None, nq, HEAD_DIM), bmap),
            pl.BlockSpec((None, 1, nq), bmap),
            pl.BlockSpec((nq, ATT_WIDTH), const),
            pl.BlockSpec((nq, nq), const),
            pl.BlockSpec((nq, nq), const),
        ],
        out_specs=pl.BlockSpec((None, nq, HEAD_DIM), bmap),
        scratch_shapes=[pltpu.VMEM((nq, 1), F32), pltpu.VMEM((nq, 1), F32), pltpu.VMEM((nq, ATT_WIDTH), F32)],
    )
    return pl.pallas_call(
        _attn_sample_kernel,
        grid_spec=grid_spec,
        out_shape=jax.ShapeDtypeStruct((db, nq, HEAD_DIM), F32),
        compiler_params=_cparams(("arbitrary", "arbitrary")),
        name="attn_sample",
    )(page_table.reshape(-1), *([cache_kt] * n), *([cache_vt] * n), fpast, ftot, qblk, q32, kn32, vn32, lfn,
      jnp.asarray(own.astype(np.float32)), jnp.asarray(nmask), jnp.asarray(ncum, dtype=BF16))


def _merge_kernel(ys_ref, ya_ref, x_ref, gate1_ref, shift2_ref, scale2_ref, gs_ref, ga_ref, g2_ref,
                  wglu_ref, bglu_ref, wos_ref, woa_ref, wr_ref, br_ref, cnt_in_ref, strict_ref,
                  x1_ref, h2_ref, eidx_ref, gate_ref, rank_ref, cnt_out_ref, cnt_scr, *, hi_prec):
    first = (pl.program_id(0) == 0) & (pl.program_id(1) == 0)

    @pl.when(first)
    def _():
        cnt_scr[...] = cnt_in_ref[...]

    mm = _dot_x3 if hi_prec else (lambda a, b: _dot(a.astype(BF16), b))
    y = jax.nn.gelu(ys_ref[...])
    y = y * jax.nn.sigmoid(mm(y, wglu_ref[...]) + bglu_ref[...])
    mix = mm(_rms(y, gs_ref[...]), wos_ref[...]) + mm(_rms(ya_ref[...], ga_ref[...]), woa_ref[...])
    x1 = x_ref[...] + gate1_ref[...] * mix
    x1_ref[...] = x1
    h2 = _rms(x1, g2_ref[...]) * (1.0 + scale2_ref[...]) + shift2_ref[...]
    h2_ref[...] = h2
    logits = _dot_x3(h2, wr_ref[...]) + br_ref[...]
    rows = logits.shape[0]
    lane = lax.broadcasted_iota(jnp.int32, (rows, LANES), 1)
    lane_f = lane.astype(F32)
    vals, idxs = [], []
    for _ in range(TOP_K):
        m = jnp.max(logits, axis=1, keepdims=True)
        idx = jnp.min(jnp.where(logits == m, lane_f, float(LANES)), axis=1, keepdims=True)
        vals.append(m)
        idxs.append(idx)
        logits = jnp.where(lane_f == idx, -jnp.inf, logits)
    exps = [jnp.exp(v - vals[0]) for v in vals]
    den = exps[0] + exps[1] + exps[2] + exps[3]
    cnt = cnt_scr[...]
    strict = strict_ref[...]
    e_out = jnp.zeros((rows, LANES), F32)
    g_out = jnp.zeros((rows, LANES), F32)
    r_out = jnp.zeros((rows, LANES), F32)
    for kk in range(TOP_K):
        onehot = lane_f == idxs[kk]
        oh = jnp.where(onehot, 1.0, 0.0)
        before = _dot(strict, oh.astype(BF16)) + cnt
        rank = jnp.sum(jnp.where(onehot, before, 0.0), axis=1, keepdims=True)
        cnt = cnt + jnp.sum(oh, axis=0, keepdims=True)
        e_out = jnp.where(lane == kk, idxs[kk], e_out)
        g_out = jnp.where(lane == kk, exps[kk] / den, g_out)
        r_out = jnp.where(lane == kk, rank, r_out)
    cnt_scr[...] = cnt
    cnt_out_ref[...] = cnt
    eidx_ref[...] = e_out.astype(jnp.int32)
    gate_ref[...] = g_out
    rank_ref[...] = r_out.astype(jnp.int32)


def _merge(ys, ya, x, gate1, shift2, scale2, g_out_ssm, g_out_att, g2, w_glu, b_glu, w_out_s, w_out_a,
           w_router, b_router, cnt_in, *, tile, hi_prec):
    s_dim, r_dim, _ = x.shape
    rm = gate1.shape[1]
    row = lambda s, t: (s, t, 0)
    const2 = lambda s, t: (0, 0)
    mod_map = (lambda s, t: (s, 0, 0)) if rm == 1 else row
    mod_blk = (None, 1, D_MODEL) if rm == 1 else (None, tile, D_MODEL)
    full = lambda a: pl.BlockSpec(a.shape, const2)
    half = pl.BlockSpec((None, tile, SSM_WIDTH), row)
    wide = pl.BlockSpec((None, tile, D_MODEL), row)
    narrow = pl.BlockSpec((None, tile, LANES), row)
    strict = _tri_matrix(tile, strict=True)
    args = (ys, ya, x, gate1, shift2, scale2, g_out_ssm, g_out_att, g2, w_glu, b_glu, w_out_s, w_out_a,
            w_router, b_router, cnt_in, strict)
    return pl.pallas_call(
        functools.partial(_merge_kernel, hi_prec=hi_prec),
        grid=(s_dim, r_dim // tile),
        in_specs=[half, half, wide, pl.BlockSpec(mod_blk, mod_map), pl.BlockSpec(mod_blk, mod_map),
                  pl.BlockSpec(mod_blk, mod_map)] + [full(a) for a in args[6:]],
        out_specs=[wide, wide, narrow, narrow, narrow, pl.BlockSpec((1, LANES), const2)],
        out_shape=[jax.ShapeDtypeStruct((s_dim, r_dim, D_MODEL), F32),
                   jax.ShapeDtypeStruct((s_dim, r_dim, D_MODEL), F32),
                   jax.ShapeDtypeStruct((s_dim, r_dim, LANES), jnp.int32),
                   jax.ShapeDtypeStruct((s_dim, r_dim, LANES), F32),
                   jax.ShapeDtypeStruct((s_dim, r_dim, LANES), jnp.int32),
                   jax.ShapeDtypeStruct((1, LANES), F32)],
        scratch_shapes=[pltpu.VMEM((1, LANES), F32)],
        compiler_params=_cparams(("arbitrary", "arbitrary")),
        name="merge_hi" if hi_prec else "merge",
    )(*args)


MOE_TILE = 256
ROW_TILE = 128
DMA_UNROLL = 8


def _dispatch_kernel(fill_ref, dest_ref, hp_ref, hs_ref, xpad_ref, zero_scr, sem):
    i = pl.program_id(0)
    last = pl.num_programs(0) - 1
    rows = hp_ref.shape[0]

    def scatter(h_ref):
        def row_copy(src_row, dst_row):
            return pltpu.make_async_copy(h_ref.at[pl.ds(src_row, 1)], xpad_ref.at[pl.ds(dst_row, 1)], sem)

        def issue(t, carry):
            for kk in range(TOP_K):
                row_copy(t, dest_ref[0, t * TOP_K + kk]).start(priority=kk % 2)
            return carry

        lax.fori_loop(0, rows, issue, 0, unroll=DMA_UNROLL // TOP_K)

        def drain(a, carry):
            row_copy(0, 0).wait()
            return carry

        lax.fori_loop(0, rows * TOP_K, drain, 0, unroll=DMA_UNROLL)

    @pl.when(i < last)
    def _():
        scatter(hp_ref)

    @pl.when(i == last)
    def _():
        scatter(hs_ref)
        zero_scr[...] = jnp.zeros_like(zero_scr)

        def zero_row(dst_row):
            return pltpu.make_async_copy(zero_scr.at[pl.ds(0, 1)], xpad_ref.at[pl.ds(dst_row, 1)], sem)

        def per_expert(e, carry):
            start, count = fill_ref[2 * e], fill_ref[2 * e + 1]

            def fill(a, c2):
                zero_row(start + a).start()
                return c2

            lax.fori_loop(0, count, fill, 0)

            def fill_wait(a, c2):
                zero_row(0).wait()
                return c2

            lax.fori_loop(0, count, fill_wait, 0)
            return carry

        lax.fori_loop(0, N_EXPERTS, per_expert, 0)

        def zero_tile(t):
            return pltpu.make_async_copy(zero_scr, xpad_ref.at[pl.ds(t * MOE_TILE, MOE_TILE)], sem)

        n_used = fill_ref[2 * N_EXPERTS]
        n_tiles = xpad_ref.shape[0] // MOE_TILE

        def tail(t, carry):
            zero_tile(t).start()
            return carry

        lax.fori_loop(n_used, n_tiles, tail, 0)

        def tail_wait(t, carry):
            zero_tile(0).wait()
            return carry

        lax.fori_loop(n_used, n_tiles, tail_wait, 0)


def _dispatch(h2p, h2s, dest, fill, cap):
    ntp = h2p.shape[0] // ROW_TILE
    nt = ntp + 1
    grid_spec = pltpu.PrefetchScalarGridSpec(
        num_scalar_prefetch=1,
        grid=(nt,),
        in_specs=[pl.BlockSpec((None, 1, ROW_TILE * TOP_K), lambda i, f: (i, 0, 0), memory_space=pltpu.SMEM),
                  pl.BlockSpec((ROW_TILE, D_MODEL), lambda i, f: (jnp.minimum(i, ntp - 1), 0)),
                  pl.BlockSpec((ROW_TILE, D_MODEL), lambda i, f: (0, 0))],
        out_specs=pl.BlockSpec(memory_space=pl.ANY),
        scratch_shapes=[pltpu.VMEM((MOE_TILE, D_MODEL), F32), pltpu.SemaphoreType.DMA(())],
    )
    return pl.pallas_call(
        _dispatch_kernel,
        grid_spec=grid_spec,
        out_shape=jax.ShapeDtypeStruct((cap, D_MODEL), F32),
        compiler_params=_cparams(("arbitrary",)),
        name="dispatch",
    )(fill, dest.reshape(nt, 1, ROW_TILE * TOP_K), h2p, h2s)


def _experts_kernel(te_ref, nu_ref, x_ref, wgu_ref, bgu_ref, wd_ref, bd_ref, y_ref, wgu_scr, wd_scr):
    i = pl.program_id(0)
    prev = te_ref[jnp.maximum(i - 1, 0)]
    fresh = (i == 0) | (te_ref[i] != prev)

    @pl.when(fresh)
    def _():
        wgu_scr[...] = wgu_ref[...].astype(BF16)
        wd_scr[...] = wd_ref[...].astype(BF16)

    @pl.when(i < nu_ref[0])
    def _():
        gu = _dot(x_ref[...].astype(BF16), wgu_scr[...]) + bgu_ref[...]
        g = jnp.minimum(gu[:, :D_EXPERT], SWIGLU_LIMIT)
        up = jnp.clip(gu[:, D_EXPERT:], -SWIGLU_LIMIT, SWIGLU_LIMIT)
        act = (up + 1.0) * (g * jax.nn.sigmoid(SWIGLU_ALPHA * g))
        y_ref[...] = _dot(act.astype(BF16), wd_scr[...]) + bd_ref[...]

    @pl.when(i >= nu_ref[0])
    def _():
        y_ref[...] = jnp.zeros_like(y_ref)


def _experts(xpad, tile_expert, n_used, w_gate_up, b_gate_up, w_down, b_down, l):
    cap = xpad.shape[0]
    n_tiles = cap // MOE_TILE
    tmap = lambda i, te, nu: (jnp.minimum(i, nu[0] - 1), 0)
    grid_spec = pltpu.PrefetchScalarGridSpec(
        num_scalar_prefetch=2,
        grid=(n_tiles,),
        in_specs=[
            pl.BlockSpec((MOE_TILE, D_MODEL), tmap),
            pl.BlockSpec((None, None, D_MODEL, 2 * D_EXPERT), lambda i, te, nu: (l, te[i], 0, 0)),
            pl.BlockSpec((None, None, 1, 2 * D_EXPERT), lambda i, te, nu: (l, te[i], 0, 0)),
            pl.BlockSpec((None, None, D_EXPERT, D_MODEL), lambda i, te, nu: (l, te[i], 0, 0)),
            pl.BlockSpec((None, None, 1, D_MODEL), lambda i, te, nu: (l, te[i], 0, 0)),
        ],
        out_specs=pl.BlockSpec((MOE_TILE, D_MODEL), lambda i, te, nu: (i, 0)),
        scratch_shapes=[pltpu.VMEM((D_MODEL, 2 * D_EXPERT), BF16), pltpu.VMEM((D_EXPERT, D_MODEL), BF16)],
    )
    return pl.pallas_call(
        _experts_kernel,
        grid_spec=grid_spec,
        out_shape=jax.ShapeDtypeStruct((cap, D_MODEL), F32),
        compiler_params=_cparams(("arbitrary",)),
        name="experts",
    )(tile_expert, n_used, xpad, w_gate_up, b_gate_up.reshape(DEPTH, N_EXPERTS, 1, 2 * D_EXPERT),
      w_down, b_down.reshape(DEPTH, N_EXPERTS, 1, D_MODEL))


def _combine_kernel(dest_ref, x1_ref, gate2_ref, g_ref, ypad_ref, o_ref, buf, sem):
    rows = x1_ref.shape[0]

    def row_copy(src_row, slot):
        return pltpu.make_async_copy(ypad_ref.at[pl.ds(src_row, 1)], buf.at[pl.ds(slot, 1)], sem)

    def issue(t, carry):
        for kk in range(TOP_K):
            row_copy(dest_ref[0, t * TOP_K + kk], kk * rows + t).start(priority=kk % 2)
        return carry

    lax.fori_loop(0, rows, issue, 0, unroll=DMA_UNROLL // TOP_K)

    def drain(a, carry):
        row_copy(0, 0).wait()
        return carry

    lax.fori_loop(0, rows * TOP_K, drain, 0, unroll=DMA_UNROLL)
    g = g_ref[...]
    moe = jnp.zeros((rows, D_MODEL), F32)
    for kk in range(TOP_K):
        moe = moe + g[:, kk:kk + 1] * buf[kk * rows:(kk + 1) * rows, :]
    o_ref[...] = x1_ref[...] + gate2_ref[...] * moe


def _combine(x1, gate2, gates, dest, ypad):
    s_dim, r_dim, _ = x1.shape
    rm = gate2.shape[1]
    nt = r_dim // ROW_TILE
    row = lambda s, t: (s, t, 0)
    mod_map = (lambda s, t: (s, 0, 0)) if rm == 1 else row
    mod_blk = (None, 1, D_MODEL) if rm == 1 else (None, ROW_TILE, D_MODEL)
    return pl.pallas_call(
        _combine_kernel,
        grid=(s_dim, nt),
        in_specs=[pl.BlockSpec((None, 1, ROW_TILE * TOP_K), lambda s, t: (s * nt + t, 0, 0),
                               memory_space=pltpu.SMEM),
                  pl.BlockSpec((None, ROW_TILE, D_MODEL), row), pl.BlockSpec(mod_blk, mod_map),
                  pl.BlockSpec((None, ROW_TILE, LANES), row), pl.BlockSpec(memory_space=pl.ANY)],
        out_specs=pl.BlockSpec((None, ROW_TILE, D_MODEL), row),
        out_shape=jax.ShapeDtypeStruct(x1.shape, F32),
        scratch_shapes=[pltpu.VMEM((TOP_K * ROW_TILE, D_MODEL), F32), pltpu.SemaphoreType.DMA(())],
        compiler_params=_cparams(("arbitrary", "arbitrary")),
        name="combine",
    )(dest.reshape(s_dim * nt, 1, ROW_TILE * TOP_K), x1, gate2, gates, ypad)


def _mods(mod_rows):
    return [m for m in jnp.split(mod_rows, N_MOD, axis=-1)]


def kernel(x_prompt, x_sample, c_prompt, c_sample, cache_k, cache_v, cache_logf, state_ssm_re, state_ssm_im,
           page_table, w_ada, b_ada, g_norm1, g_norm2, w_in, b_forget, g_q, g_k, ssm_lam_re, ssm_lam_im,
           ssm_log_dt, ssm_b_re, ssm_b_im, ssm_c_re, ssm_c_im, ssm_d, w_glu, b_glu, g_out_ssm, g_out_att, w_out,
           w_router, b_router, w_gate_up, b_gate_up, w_down, b_down):
    bsz, seq, _ = x_prompt.shape
    db, t_new, _ = x_sample.shape
    n_pool = cache_k.shape[1]
    n_seq = bsz + db
    c_rows = -(-n_seq // 8) * 8
    c_all = jnp.concatenate([c_prompt, c_sample, jnp.zeros((c_rows - n_seq, D_MODEL), F32)], axis=0)
    mod = _adaln(c_all, w_ada, b_ada)
    prep = _ssm_prep(ssm_lam_re, ssm_lam_im, ssm_log_dt, ssm_b_re, ssm_b_im, ssm_c_re, ssm_c_im, ssm_d)
    logf_view = cache_logf.reshape(DEPTH, n_pool, FLAT)
    cache_kt = jnp.transpose(cache_k, (0, 1, 3, 4, 2))
    cache_vt = jnp.transpose(cache_v, (0, 1, 3, 4, 2))
    n_pages = page_table.shape[1]
    st_re = state_ssm_re.reshape(DEPTH, db, SSM_GROUPS * SSM_STATE)
    st_im = state_ssm_im.reshape(DEPTH, db, SSM_GROUPS * SSM_STATE)
    n_tok_p, n_tok_s = bsz * seq, db * t_new
    assert n_tok_s == ROW_TILE and n_tok_p % ROW_TILE == 0
    n_tiles = -(-(n_tok_p + n_tok_s) * TOP_K // MOE_TILE) + N_EXPERTS
    cap = n_tiles * MOE_TILE
    pad_f = LANES - ATT_HEADS
    pad_e = LANES - N_EXPERTS

    xp = x_prompt
    xs = jnp.swapaxes(x_sample, 0, 1)
    outs = {name: [] for name in ('kp', 'vp', 'lfp', 'srp', 'sip', 'ks', 'vs', 'lfs', 'srs', 'sis')}
    for l in range(DEPTH):
        mp = [m[:, None, :] for m in _mods(mod[l, :bsz])]
        ms = [jnp.broadcast_to(m[None], (t_new, db, D_MODEL)) for m in _mods(mod[l, bsz:n_seq])]
        w_main = w_in[l, :, :IN_MAIN]
        w_f = jnp.pad(w_in[l, :, IN_MAIN:], ((0, 0), (0, pad_f)))
        b_f = jnp.pad(b_forget[l], (0, pad_f))[None]
        gq = jnp.tile(g_q[l], ATT_HEADS)[None]
        gk = jnp.tile(g_k[l], ATT_HEADS)[None]
        g1 = g_norm1[l][None]
        g2 = g_norm2[l][None]
        w_os, w_oa = w_out[l, :SSM_WIDTH], w_out[l, SSM_WIDTH:]
        w_r = jnp.pad(w_router[l], ((0, 0), (0, pad_e)))
        b_r = jnp.pad(b_router[l], (0, pad_e), constant_values=NEG_BIG)[None]
        small = (g_out_ssm[l][None], g_out_att[l][None], g2)

        u, q, k, v, kcat, vcat, lf, fc = _in_proj(xp, mp[0], mp[1], g1, w_main.astype(BF16), w_f.astype(BF16),
                                                  b_f, gq, gk, tile=512, hi_prec=False, with_cumsum=True)
        y_ssm, s_re, s_im = _ssm_prompt(u, prep, l)
        bound = float(HEAD_DIM) * ATT_SCALE * jnp.max(jnp.abs(g_q[l])) * jnp.max(jnp.abs(g_k[l]))
        y_att = lax.cond(
            bound <= SHIFT_LIMIT,
            lambda: _attn_bounded(q, kcat, vcat, fc - bound * LOG2E),
            lambda: _attn_prompt(q, kcat, vcat, fc, jnp.swapaxes(fc[..., :ATT_HEADS], 1, 2)))
        x1p, h2p, e_p, gate_p, rank_p, cnt_p = _merge(
            y_ssm, y_att, xp, mp[2], mp[3], mp[4], *small, w_glu[l].astype(BF16), b_glu[l][None],
            w_os.astype(BF16), w_oa.astype(BF16), w_r, b_r, jnp.zeros((1, LANES), F32), tile=512, hi_prec=False)
        outs['kp'].append(k.reshape(bsz, seq, ATT_HEADS, HEAD_DIM))
        outs['vp'].append(v.reshape(bsz, seq, ATT_HEADS, HEAD_DIM))
        outs['lfp'].append(lf[..., :ATT_HEADS])
        outs['srp'].append(s_re)
        outs['sip'].append(s_im)

        u_s, q_s, k_s, v_s, _, _, lf_s, _ = _in_proj(xs, ms[0], ms[1], g1, w_main, w_f, b_f, gq, gk,
                                                     tile=db, hi_prec=True, with_cumsum=False)
        y_ssm_s, o_re, o_im = _ssm_sample(u_s, st_re, st_im, prep, l)
        per_seq = lambda a: jnp.swapaxes(a, 0, 1).reshape(db, t_new * ATT_HEADS, HEAD_DIM)
        lfn = jnp.swapaxes(lf_s[..., :ATT_HEADS], 0, 1).reshape(db, 1, t_new * ATT_HEADS)
        fpast = _fpast(logf_view, page_table, l).reshape(db, n_pages, ATT_HEADS, PAGE_SIZE)
        ftot = jnp.tile(fpast[:, -1, :, -1], (1, t_new)).reshape(db, 1, t_new * ATT_HEADS)
        o_s = _attn_sample(per_seq(q_s), per_seq(k_s), per_seq(v_s), lfn, cache_kt, cache_vt, fpast, ftot,
                           page_table, l)
        y_att_s = jnp.swapaxes(o_s.reshape(db, t_new, ATT_WIDTH), 0, 1)
        x1s, h2s, e_s, gate_s, rank_s, cnt = _merge(
            y_ssm_s, y_att_s, xs, ms[2], ms[3], ms[4], *small, w_glu[l], b_glu[l][None], w_os, w_oa, w_r, b_r,
            cnt_p, tile=db, hi_prec=True)
        outs['ks'].append(jnp.swapaxes(k_s, 0, 1).reshape(db, t_new, ATT_HEADS, HEAD_DIM))
        outs['vs'].append(jnp.swapaxes(v_s, 0, 1).reshape(db, t_new, ATT_HEADS, HEAD_DIM))
        outs['lfs'].append(jnp.swapaxes(lf_s[..., :ATT_HEADS], 0, 1))
        outs['srs'].append(o_re.reshape(db, SSM_GROUPS, SSM_STATE))
        outs['sis'].append(o_im.reshape(db, SSM_GROUPS, SSM_STATE))

        counts = cnt[0, :N_EXPERTS].astype(jnp.int32)
        padded = (counts + MOE_TILE - 1) // MOE_TILE * MOE_TILE
        pend = jnp.cumsum(padded)
        pstart = pend - padded
        experts = jnp.arange(N_EXPERTS, dtype=jnp.int32)

        def slots(e, rank):
            hit = e[..., :TOP_K, None] == experts
            return (jnp.sum(jnp.where(hit, pstart, 0), axis=-1) + rank[..., :TOP_K]).reshape(-1)

        dest_p = slots(e_p, rank_p)
        dest_s = slots(e_s, rank_s)
        tile_start = jnp.arange(n_tiles, dtype=jnp.int32) * MOE_TILE
        tile_expert = jnp.sum((pend[None, :] <= tile_start[:, None]).astype(jnp.int32), axis=1)
        tile_expert = jnp.minimum(tile_expert, N_EXPERTS - 1)
        n_used = (pend[-1:] // MOE_TILE).astype(jnp.int32)
        fill = jnp.stack([pstart + counts, padded - counts], axis=-1).reshape(-1)
        fill = jnp.concatenate([fill, n_used]).astype(jnp.int32)
        xpad = _dispatch(h2p.reshape(n_tok_p, D_MODEL), h2s.reshape(n_tok_s, D_MODEL),
                         jnp.concatenate([dest_p, dest_s]), fill, cap)
        ypad = _experts(xpad, tile_expert, n_used, w_gate_up, b_gate_up, w_down, b_down, l)
        xp = _combine(x1p, mp[5], gate_p, dest_p, ypad)
        xs = _combine(x1s.reshape(1, n_tok_s, D_MODEL), ms[5].reshape(1, n_tok_s, D_MODEL),
                      gate_s.reshape(1, n_tok_s, LANES), dest_s, ypad).reshape(t_new, db, D_MODEL)

    st = lambda name: jnp.stack(outs[name])
    return (xp, jnp.swapaxes(xs, 0, 1), st('kp'), st('vp'), st('lfp'), st('srp'), st('sip'),
            st('ks'), st('vs'), st('lfs'), st('srs'), st('sis'))
```
